```python
import math
import jax, jax.numpy as jnp
from jax import lax
import numpy as np

D_MODEL = 1024
BATCH = 2
SEQ = 8192
DEPTH = 1

BLOCK = 128
N_META = 16
N_LEAD = BLOCK
FIRST_VALID = N_LEAD - N_META
HEAD_DIM = 64
BRANCH_WIDTH = D_MODEL // 2
N_BRANCH = 2
SB_HEADS = BRANCH_WIDTH // HEAD_DIM
SW_HEADS = BRANCH_WIDTH // HEAD_DIM
SW_KV_HEADS = 2
SW_GROUP = SW_HEADS // SW_KV_HEADS
SW_KV_WIDTH = SW_KV_HEADS * HEAD_DIM
WINDOW = 128
N_BUCKETS = 32
MAX_DISTANCE = 128
RMS_EPS = 1e-6
SPLIT_SIZES = (
    BRANCH_WIDTH, BRANCH_WIDTH, BRANCH_WIDTH, BRANCH_WIDTH,
    BRANCH_WIDTH, SW_KV_WIDTH, SW_KV_WIDTH, BRANCH_WIDTH,
    N_BRANCH * D_MODEL,
)
PROJ_WIDTH = sum(SPLIT_SIZES)
SPLIT_POINTS = tuple(int(i) for i in np.cumsum(SPLIT_SIZES)[:-1])

kernel_name = "hybrid_stickbreak_swa_sink_block"


def _rmsnorm(x, w):
    xf = x.astype(jnp.float32)
    xf = xf * lax.rsqrt(jnp.mean(xf * xf, axis=-1, keepdims=True) + RMS_EPS)
    return (xf * w.astype(jnp.float32)).astype(x.dtype)


def _t5_buckets(rel):
    n = np.maximum(rel, 0)
    max_exact = N_BUCKETS // 2
    large = max_exact + (np.log(np.maximum(n, 1) / max_exact)
                         / math.log(MAX_DISTANCE / max_exact)
                         * (N_BUCKETS - max_exact)).astype(np.int32)
    large = np.minimum(large, N_BUCKETS - 1)
    return np.where(n < max_exact, n, large).astype(np.int32)


def _stick_breaking(q, k, v):
    b, h, lp, dh = q.shape
    nb = lp // BLOCK
    kpos = jnp.arange(lp)
    key_ok = kpos >= FIRST_VALID
    scale = dh ** -0.5
    qblocks = q.reshape(b, h, nb, BLOCK, dh).transpose(2, 0, 1, 3, 4)
    qpos = jnp.arange(lp).reshape(nb, BLOCK)

    def one_block(args):
        qb, qp = args
        z = jnp.einsum('bhqd,bhkd->bhqk', qb, k, preferred_element_type=jnp.float32) * scale
        visible = (kpos[None, :] < qp[:, None]) & key_ok[None, :]
        log_1m = jnp.where(visible, jax.nn.log_sigmoid(-z), 0.0)
        later = lax.cumsum(log_1m, axis=3, reverse=True) - log_1m
        w = jnp.where(visible, jnp.exp(jax.nn.log_sigmoid(z) + later), 0.0)
        return jnp.einsum('bhqk,bhkd->bhqd', w.astype(v.dtype), v,
                          preferred_element_type=jnp.float32)

    out = lax.map(one_block, (qblocks, qpos))
    return out.transpose(1, 2, 0, 3, 4).reshape(b, h, lp, dh)


def _sliding_window_gqa(q, k, v, q_gain, k_gain, sinks, rel_bias):
    b, _, lp, dh = q.shape
    nb = lp // BLOCK
    q = _rmsnorm(q, q_gain)
    k = _rmsnorm(k, k_gain)
    qb = q.reshape(b, SW_KV_HEADS, SW_GROUP, nb, BLOCK, dh)

    def band(t):
        t = jnp.pad(t, ((0, 0), (0, 0), (BLOCK, 0), (0, 0))).reshape(b, SW_KV_HEADS, nb + 1, BLOCK, dh)
        return jnp.concatenate([t[:, :, :-1], t[:, :, 1:]], axis=3)

    kb, vb = band(k), band(v)
    logits = jnp.einsum('bkgnqd,bknsd->bkgnqs', qb, kb,
                        preferred_element_type=jnp.float32) * dh ** -0.5
    rel = (BLOCK + np.arange(BLOCK))[:, None] - np.arange(2 * BLOCK)[None, :]
    bias = rel_bias.astype(jnp.float32)[_t5_buckets(rel)]
    bias = jnp.transpose(bias, (2, 0, 1)).reshape(SW_KV_HEADS, SW_GROUP, 1, BLOCK, 2 * BLOCK)
    kpos = (np.arange(nb)[:, None, None] * BLOCK - BLOCK + np.arange(2 * BLOCK)[None, None, :])
    visible = (rel >= 0)[None] & (rel < WINDOW)[None] & (kpos >= FIRST_VALID)
    logits = jnp.where(visible, logits + bias, -jnp.inf)
    sink = sinks.astype(jnp.float32).reshape(SW_KV_HEADS, SW_GROUP, 1, 1, 1)
    m = jnp.maximum(jnp.max(logits, axis=-1, keepdims=True), sink)
    p = jnp.exp(logits - m)
    denom = jnp.sum(p, axis=-1, keepdims=True) + jnp.exp(sink - m)
    out = jnp.einsum('bkgnqs,bknsd->bkgnqd', (p / denom).astype(v.dtype), vb,
                     preferred_element_type=jnp.float32)
    return out.reshape(b, SW_HEADS, lp, dh)


def _heads(t, n):
    b, l, _ = t.shape
    return t.reshape(b, l, n, HEAD_DIM).transpose(0, 2, 1, 3)


def _merge_heads(t):
    b, h, l, d = t.shape
    return t.transpose(0, 2, 1, 3).reshape(b, l, h * d)


def _layer(x, norm_w, w_in, q_gain, k_gain, sinks, w_branch, w_out, rel_bias):
    b, lp, _ = x.shape
    xn = _rmsnorm(x, norm_w)
    proj = jnp.einsum('bld,dp->blp', xn, w_in)
    qa, ka, va, za, qbq, kbk, vbv, zb, gates = jnp.split(proj, SPLIT_POINTS, axis=-1)
    oa = _merge_heads(_stick_breaking(_heads(qa, SB_HEADS), _heads(ka, SB_HEADS),
                                      _heads(va, SB_HEADS))).astype(x.dtype)
    ob = _merge_heads(_sliding_window_gqa(_heads(qbq, SW_HEADS), _heads(kbk, SW_KV_HEADS),
                                          _heads(vbv, SW_KV_HEADS), q_gain, k_gain,
                                          sinks, rel_bias)).astype(x.dtype)
    branches = jnp.stack([oa * jax.nn.silu(za), ob * jax.nn.silu(zb)], axis=2)
    y = jnp.einsum('blgc,gcd->blgd', branches, w_branch)
    g = jax.nn.sigmoid(gates.reshape(b, lp, N_BRANCH, D_MODEL))
    merged = jnp.sum(g * y, axis=2)
    return x + jnp.einsum('bld,de->ble', merged, w_out)


def setup_inputs(seed: int = 0) -> dict:
    key = jax.random.key(seed)
    ks = jax.random.split(key, 11)
    f32 = jnp.float32
    x = jax.random.normal(ks[0], (BATCH, SEQ, D_MODEL), f32)
    meta = jax.random.normal(ks[1], (N_META, D_MODEL), f32)
    rel_bias = 0.5 * jax.random.normal(ks[2], (N_BUCKETS, SW_HEADS), f32)
    norm_w = 1.0 + 0.05 * jax.random.normal(ks[3], (DEPTH, D_MODEL), f32)
    w_in = jax.random.normal(ks[4], (DEPTH, D_MODEL, PROJ_WIDTH), f32) * D_MODEL ** -0.5
    q_gain = 1.0 + 0.05 * jax.random.normal(ks[5], (DEPTH, HEAD_DIM), f32)
    k_gain = 1.0 + 0.05 * jax.random.normal(ks[6], (DEPTH, HEAD_DIM), f32)
    sinks = 0.5 * jax.random.normal(ks[7], (DEPTH, SW_HEADS), f32)
    w_branch = jax.random.normal(ks[8], (DEPTH, N_BRANCH, BRANCH_WIDTH, D_MODEL), f32) * BRANCH_WIDTH ** -0.5
    w_out = jax.random.normal(ks[9], (DEPTH, D_MODEL, D_MODEL), f32) * D_MODEL ** -0.5
    return {"x": x, "meta": meta, "rel_bias": rel_bias, "norm_w": norm_w, "w_in": w_in,
            "q_gain": q_gain, "k_gain": k_gain, "sinks": sinks, "w_branch": w_branch,
            "w_out": w_out}


def reference(x, meta, rel_bias, norm_w, w_in, q_gain, k_gain, sinks, w_branch, w_out):
    b = x.shape[0]
    pad = jnp.zeros((b, FIRST_VALID, D_MODEL), x.dtype)
    h = jnp.concatenate([pad, jnp.broadcast_to(meta.astype(x.dtype)[None], (b, N_META, D_MODEL)), x], axis=1)
    for layer in range(DEPTH):
        h = _layer(h, norm_w[layer], w_in[layer], q_gain[layer], k_gain[layer], sinks[layer],
                   w_branch[layer], w_out[layer], rel_bias)
    return h[:, N_LEAD:].astype(x.dtype)
```

```python
import functools
import math

import jax
import jax.numpy as jnp
import numpy as np
from jax import lax
from jax.experimental import pallas as pl
from jax.experimental.pallas import tpu as pltpu

D_MODEL = 1024
HEAD_DIM = 64
BRANCH_WIDTH = 512
SW_KV_WIDTH = 128
N_META = 16
WINDOW = 128
SW_BLOCK = 128
N_BUCKETS = 32
MAX_DISTANCE = 128
RMS_EPS = 1e-6

LANES = 128
TILE = 256
LEAD = 2 * TILE
FIRST_VALID = LEAD - N_META
ROW_TILE = 512
QKV_WIDTH = 4 * BRANCH_WIDTH + 4 * SW_KV_WIDTH
GATE_WIDTH = 2 * BRANCH_WIDTH + 2 * D_MODEL
EXP_ZERO_BELOW = -104.0
VMEM_LIMIT = 48 * 1024 * 1024

_NT = (((1,), (1,)), ((), ()))


def _rms_rows(x, w):
    ms = jnp.mean(x * x, axis=-1, keepdims=True)
    return x * lax.rsqrt(ms + RMS_EPS) * w


def _split_bf16(x):
    hi = x.astype(jnp.bfloat16)
    lo = (x - hi.astype(jnp.float32)).astype(jnp.bfloat16)
    return hi, lo


def _qkv_kernel(x_ref, meta_ref, nw_ref, w_ref, gq_ref, gk_ref, o_ref):
    i = pl.program_id(1)
    r = lax.broadcasted_iota(jnp.int32, (LANES, LANES), 0) // HEAD_DIM
    c = lax.broadcasted_iota(jnp.int32, (LANES, LANES), 1) // HEAD_DIM
    head_ones = jnp.where(r == c, 1.0, 0.0).astype(jnp.bfloat16)

    def head_norm(a, gain):
        hi, lo = _split_bf16(a * a)
        ssq = (jnp.dot(hi, head_ones, preferred_element_type=jnp.float32)
               + jnp.dot(lo, head_ones, preferred_element_type=jnp.float32))
        return a * lax.rsqrt(ssq * (1.0 / HEAD_DIM) + RMS_EPS) * gain

    def project(rows, r0, nrows):
        xn = _rms_rows(rows, nw_ref[...]).astype(jnp.bfloat16)
        for cb in range(QKV_WIDTH // LANES):
            c0 = cb * LANES
            a = jnp.dot(xn, w_ref[:, c0:c0 + LANES], preferred_element_type=jnp.float32)
            if c0 < BRANCH_WIDTH:
                a = a * (HEAD_DIM ** -0.5)
            elif 3 * BRANCH_WIDTH <= c0 < 4 * BRANCH_WIDTH:
                a = head_norm(a, gq_ref[...])
            elif 4 * BRANCH_WIDTH <= c0 < 4 * BRANCH_WIDTH + 2 * SW_KV_WIDTH:
                a = head_norm(a, gk_ref[...])
            o_ref[0, r0:r0 + nrows, c0:c0 + LANES] = a.astype(o_ref.dtype)

    @pl.when(i == 0)
    def _():
        o_ref[0, :FIRST_VALID, :] = jnp.zeros((FIRST_VALID, QKV_WIDTH), o_ref.dtype)
        project(meta_ref[...], FIRST_VALID, N_META)

    @pl.when(i > 0)
    def _():
        project(x_ref[0], 0, ROW_TILE)


def _qkv_project(x, meta, norm_w, w_qkv, gq, gk):
    b, seq, d = x.shape
    assert LEAD == ROW_TILE and seq % ROW_TILE == 0
    n_tiles = (seq + LEAD) // ROW_TILE
    const = lambda bi, i: (0, 0)
    return pl.pallas_call(
        _qkv_kernel,
        grid=(b, n_tiles),
        in_specs=[
            pl.BlockSpec((1, ROW_TILE, d), lambda bi, i: (bi, jnp.maximum(i - 1, 0), 0)),
            pl.BlockSpec((N_META, d), const),
            pl.BlockSpec((1, d), const),
            pl.BlockSpec((d, QKV_WIDTH), const),
            pl.BlockSpec((1, LANES), const),
            pl.BlockSpec((1, LANES), const),
        ],
        out_specs=pl.BlockSpec((1, ROW_TILE, QKV_WIDTH), lambda bi, i: (bi, i, 0)),
        out_shape=jax.ShapeDtypeStruct((b, seq + LEAD, QKV_WIDTH), jnp.bfloat16),
        compiler_params=pltpu.CompilerParams(
            dimension_semantics=("arbitrary", "arbitrary"), vmem_limit_bytes=VMEM_LIMIT),
        name="qkv_project",
    )(x, meta, norm_w, w_qkv, gq, gk)


def _sb_kernel(q_ref, k_ref, v_ref, o_ref, acc_ref, carry_ref):
    qt = pl.program_id(2) + LEAD // TILE
    lane = lax.broadcasted_iota(jnp.int32, (1, LANES), 1)
    head_lanes = (lane < HEAD_DIM, lane >= HEAD_DIM)
    q2 = q_ref[0]
    qm = [jnp.where(m, q2, jnp.zeros_like(q2)) for m in head_lanes]
    row = lax.broadcasted_iota(jnp.int32, (TILE, TILE), 0)
    col = lax.broadcasted_iota(jnp.int32, (TILE, TILE), 1)
    later_sum = jnp.where(row > col, 1.0, 0.0).astype(jnp.bfloat16)
    qpos = qt * TILE + row

    acc_ref[...] = jnp.zeros_like(acc_ref)
    carry_ref[...] = jnp.zeros_like(carry_ref)

    def tile(j):
        start = pl.multiple_of(j * TILE, TILE)
        k2 = k_ref[0, pl.ds(start, TILE), :]
        v2 = v_ref[0, pl.ds(start, TILE), :]
        kpos = j * TILE + col
        visible = (kpos < qpos) & (kpos >= FIRST_VALID)
        upd = jnp.zeros((TILE, LANES), jnp.float32)
        for h in range(2):
            z = lax.dot_general(qm[h], k2, _NT, preferred_element_type=jnp.float32)
            log_1m = -(jnp.maximum(z, 0.0) + jnp.log1p(jnp.exp(-jnp.abs(z))))
            log_1m = jnp.where(visible, log_1m, 0.0)
            hi, lo = _split_bf16(log_1m)
            later = (jnp.dot(hi, later_sum, preferred_element_type=jnp.float32)
                     + jnp.dot(lo, later_sum, preferred_element_type=jnp.float32))
            total = later[:, :1] + log_1m[:, :1]
            carry = carry_ref[h]
            w = jnp.where(visible, jnp.exp(log_1m + z + later + carry), 0.0)
            vm = jnp.where(head_lanes[h], v2, jnp.zeros_like(v2))
            upd = upd + jnp.dot(w.astype(jnp.bfloat16), vm, preferred_element_type=jnp.float32)
            carry_ref[h] = carry + total
        acc_ref[...] += upd

    def cond(state):
        j, worst = state
        return (j >= FIRST_VALID // TILE) & (worst > EXP_ZERO_BELOW)

    def body(state):
        j, _ = state
        tile(j)
        return j - 1, jnp.max(carry_ref[...])

    lax.while_loop(cond, body, (qt, jnp.float32(0.0)))
    o_ref[0] = acc_ref[...]


def _stick_breaking(qkv, b, seq):
    lp = seq + LEAD
    kv_spec = lambda cb0: pl.BlockSpec((1, lp, LANES), lambda bi, p, i: (bi, 0, cb0 + p))
    pairs = BRANCH_WIDTH // LANES
    return pl.pallas_call(
        _sb_kernel,
        grid=(b, pairs, seq // TILE),
        in_specs=[
            pl.BlockSpec((1, TILE, LANES), lambda bi, p, i: (bi, i + LEAD // TILE, p)),
            kv_spec(pairs),
            kv_spec(2 * pairs),
        ],
        out_specs=pl.BlockSpec((1, TILE, LANES), lambda bi, p, i: (bi, i, p)),
        out_shape=jax.ShapeDtypeStruct((b, seq, BRANCH_WIDTH), jnp.float32),
        scratch_shapes=[pltpu.VMEM((TILE, LANES), jnp.float32),
                        pltpu.VMEM((2, TILE, 1), jnp.float32)],
        compiler_params=pltpu.CompilerParams(
            dimension_semantics=("arbitrary", "arbitrary", "arbitrary"),
            vmem_limit_bytes=VMEM_LIMIT),
        name="stick_breaking",
    )(qkv, qkv, qkv)


def _t5_bucket_table():
    rel = (SW_BLOCK + np.arange(SW_BLOCK))[:, None] - np.arange(2 * SW_BLOCK)[None, :]
    n = np.maximum(rel, 0)
    max_exact = N_BUCKETS // 2
    large = max_exact + (np.log(np.maximum(n, 1) / max_exact)
                         / math.log(MAX_DISTANCE / max_exact)
                         * (N_BUCKETS - max_exact)).astype(np.int32)
    large = np.minimum(large, N_BUCKETS - 1)
    return np.where(n < max_exact, n, large).astype(np.int32)


def _swa_kernel(relb_ref, sink_ref, bucket_ref, q_ref, kvp_ref, kvc_ref, o_ref, bias_ref):
    n = pl.program_id(1)
    n_heads = BRANCH_WIDTH // HEAD_DIM
    group = n_heads // (SW_KV_WIDTH // HEAD_DIM)

    @pl.when((pl.program_id(0) == 0) & (n == 0))
    def _():
        bucket = bucket_ref[...]
        for h in range(n_heads):
            t = jnp.zeros(bucket.shape, jnp.float32)
            for bkt in range(N_BUCKETS):
                t = jnp.where(bucket == bkt, relb_ref[bkt, h], t)
            bias_ref[h] = t

    r = lax.broadcasted_iota(jnp.int32, (SW_BLOCK, 2 * SW_BLOCK), 0)
    c = lax.broadcasted_iota(jnp.int32, (SW_BLOCK, 2 * SW_BLOCK), 1)
    rel = SW_BLOCK + r - c
    kpos = (n + LEAD // SW_BLOCK - 1) * SW_BLOCK + c
    visible = (rel >= 0) & (rel < WINDOW) & (kpos >= FIRST_VALID)
    lane = lax.broadcasted_iota(jnp.int32, (1, LANES), 1)
    head_lanes = (lane < HEAD_DIM, lane >= HEAD_DIM)

    kv = jnp.concatenate([kvp_ref[0], kvc_ref[0]], axis=0)
    k_var = (kv[:, 0:LANES], kv[:, LANES:2 * LANES])
    v_var = (kv[:, 2 * LANES:3 * LANES], kv[:, 3 * LANES:4 * LANES])

    for p in range(BRANCH_WIDTH // LANES):
        q2 = q_ref[0, :, p * LANES:(p + 1) * LANES]
        out = jnp.zeros((SW_BLOCK, LANES), jnp.float32)
        for par in range(2):
            h = 2 * p + par
            variant = 0 if (h // group) == par else 1
            qm = jnp.where(head_lanes[par], q2, jnp.zeros_like(q2))
            logits = lax.dot_general(qm, k_var[variant], _NT, preferred_element_type=jnp.float32)
            logits = jnp.where(visible, logits + bias_ref[h], -jnp.inf)
            sink = sink_ref[h]
            m = jnp.maximum(jnp.max(logits, axis=-1, keepdims=True), sink)
            e = jnp.exp(logits - m)
            denom = jnp.sum(e, axis=-1, keepdims=True) + jnp.exp(sink - m)
            pn = (e / denom).astype(jnp.bfloat16)
            vm = jnp.where(head_lanes[par], v_var[variant], jnp.zeros_like(v_var[variant]))
            out = out + jnp.dot(pn, vm, preferred_element_type=jnp.float32)
        o_ref[0, :, p * LANES:(p + 1) * LANES] = out


def _sliding_window(qkv, rel_bias, sinks, b, seq):
    lead_blocks = LEAD // SW_BLOCK
    qb_block = 3 * BRANCH_WIDTH // BRANCH_WIDTH
    kv_block = 4 * BRANCH_WIDTH // BRANCH_WIDTH
    assert 4 * SW_KV_WIDTH == BRANCH_WIDTH
    bucket = jnp.asarray(_t5_bucket_table())
    smem = pl.BlockSpec(memory_space=pltpu.SMEM)
    return pl.pallas_call(
        _swa_kernel,
        grid=(b, seq // SW_BLOCK),
        in_specs=[
            smem, smem,
            pl.BlockSpec((SW_BLOCK, 2 * SW_BLOCK), lambda bi, n: (0, 0)),
            pl.BlockSpec((1, SW_BLOCK, BRANCH_WIDTH), lambda bi, n: (bi, n + lead_blocks, qb_block)),
            pl.BlockSpec((1, SW_BLOCK, BRANCH_WIDTH), lambda bi, n: (bi, n + lead_blocks - 1, kv_block)),
            pl.BlockSpec((1, SW_BLOCK, BRANCH_WIDTH), lambda bi, n: (bi, n + lead_blocks, kv_block)),
        ],
        out_specs=pl.BlockSpec((1, SW_BLOCK, BRANCH_WIDTH), lambda bi, n: (bi, n, 0)),
        out_shape=jax.ShapeDtypeStruct((b, seq, BRANCH_WIDTH), jnp.float32),
        scratch_shapes=[pltpu.VMEM((BRANCH_WIDTH // HEAD_DIM, SW_BLOCK, 2 * SW_BLOCK), jnp.float32)],
        compiler_params=pltpu.CompilerParams(
            dimension_semantics=("arbitrary", "arbitrary"), vmem_limit_bytes=VMEM_LIMIT),
        name="sliding_window",
    )(rel_bias, sinks, bucket, qkv, qkv, qkv)


def _epilogue_kernel(x_ref, oa_ref, ob_ref, nw_ref, wg_ref, wb_ref, wo_ref, o_ref):
    x = x_ref[...]
    xn = _rms_rows(x, nw_ref[...]).astype(jnp.bfloat16)
    merged = None
    for g, o_branch in enumerate((oa_ref, ob_ref)):
        z = jnp.dot(xn, wg_ref[:, g * BRANCH_WIDTH:(g + 1) * BRANCH_WIDTH],
                    preferred_element_type=jnp.float32)
        u = (o_branch[...] * (z * jax.nn.sigmoid(z))).astype(jnp.bfloat16)
        y = jnp.dot(u, wb_ref[g], preferred_element_type=jnp.float32)
        g0 = 2 * BRANCH_WIDTH + g * D_MODEL
        gate = jnp.dot(xn, wg_ref[:, g0:g0 + D_MODEL], preferred_element_type=jnp.float32)
        term = jax.nn.sigmoid(gate) * y
        merged = term if merged is None else merged + term
    o_ref[...] = x + jnp.dot(merged.astype(jnp.bfloat16), wo_ref[...],
                             preferred_element_type=jnp.float32)


def _epilogue(x2, oa2, ob2, norm_w, w_gate, w_branch, w_out):
    rows, d = x2.shape
    tm = TILE
    const2 = lambda i: (0, 0)
    return pl.pallas_call(
        _epilogue_kernel,
        grid=(rows // tm,),
        in_specs=[
            pl.BlockSpec((tm, d), lambda i: (i, 0)),
            pl.BlockSpec((tm, BRANCH_WIDTH), lambda i: (i, 0)),
            pl.BlockSpec((tm, BRANCH_WIDTH), lambda i: (i, 0)),
            pl.BlockSpec((1, d), const2),
            pl.BlockSpec((d, GATE_WIDTH), const2),
            pl.BlockSpec((2, BRANCH_WIDTH, d), lambda i: (0, 0, 0)),
            pl.BlockSpec((d, d), const2),
        ],
        out_specs=pl.BlockSpec((tm, d), lambda i: (i, 0)),
        out_shape=jax.ShapeDtypeStruct((rows, d), jnp.float32),
        compiler_params=pltpu.CompilerParams(
            dimension_semantics=("arbitrary",), vmem_limit_bytes=VMEM_LIMIT),
        name="epilogue",
    )(x2, oa2, ob2, norm_w, w_gate, w_branch, w_out)


def kernel(x, meta, rel_bias, norm_w, w_in, q_gain, k_gain, sinks, w_branch, w_out):
    b, seq, d = x.shape
    assert d == D_MODEL and norm_w.shape[0] == 1, "single-layer block of width 1024 only"
    assert meta.shape == (N_META, d) and seq % ROW_TILE == 0
    bw, kvw = BRANCH_WIDTH, SW_KV_WIDTH
    w = w_in[0]
    o_qb = 4 * bw
    o_kb, o_vb, o_zb, o_g = o_qb + bw, o_qb + bw + kvw, o_qb + bw + 2 * kvw, o_qb + 2 * bw + 2 * kvw
    swap = lambda t: jnp.concatenate([t[:, HEAD_DIM:], t[:, :HEAD_DIM]], axis=1)
    kb, vb = w[:, o_kb:o_kb + kvw], w[:, o_vb:o_vb + kvw]
    w_qkv = jnp.concatenate([w[:, :3 * bw], w[:, o_qb:o_qb + bw], kb, swap(kb), vb, swap(vb)],
                            axis=1).astype(jnp.bfloat16)
    w_gate = jnp.concatenate([w[:, 3 * bw:4 * bw], w[:, o_zb:o_zb + bw], w[:, o_g:]],
                             axis=1).astype(jnp.bfloat16)
    gq = jnp.tile(q_gain[0], 2)[None, :] * (HEAD_DIM ** -0.5)
    gk = jnp.tile(k_gain[0], 2)[None, :]
    nw = norm_w[0][None, :]

    qkv = _qkv_project(x, meta.astype(x.dtype), nw, w_qkv, gq, gk)
    oa = _stick_breaking(qkv, b, seq)
    ob = _sliding_window(qkv, rel_bias, sinks[0], b, seq)
    out = _epilogue(x.reshape(b * seq, d), oa.reshape(b * seq, bw), ob.reshape(b * seq, bw), nw,
                    w_gate, w_branch[0].astype(jnp.bfloat16), w_out[0].astype(jnp.bfloat16))
    return out.reshape(b, seq, d)
```

```python
import math

import jax
import jax.numpy as jnp
import numpy as np
from jax import lax
from jax.experimental import pallas as pl
from jax.experimental.pallas import tpu as pltpu

D_MODEL = 1024
HEAD_DIM = 64
BRANCH_WIDTH = 512
SW_KV_WIDTH = 128
N_HEADS = BRANCH_WIDTH // HEAD_DIM
N_META = 16
WINDOW = 128
SW_BLOCK = 128
N_BUCKETS = 32
MAX_DISTANCE = 128
RMS_EPS = 1e-6

LANES = 128
TILE = 256
LEAD = 2 * TILE
FIRST_VALID = LEAD - N_META
ROW_TILE = 512
N_PAIRS = BRANCH_WIDTH // LANES
QKV_PROJ_WIDTH = 4 * BRANCH_WIDTH + 2 * SW_KV_WIDTH
QKV_BLOCKS = 4 * N_PAIRS + 4
BLK_QA, BLK_KA, BLK_VA, BLK_QB, BLK_KVB = 0, N_PAIRS, 2 * N_PAIRS, 3 * N_PAIRS, 4 * N_PAIRS
GATE_WIDTH = 2 * BRANCH_WIDTH + 2 * D_MODEL
EXP_ZERO_BELOW = -104.0
VMEM_LIMIT = 56 * 1024 * 1024

_NT = (((1,), (1,)), ((), ()))


def _rms_rows(x, w):
    ms = jnp.mean(x * x, axis=-1, keepdims=True)
    return x * lax.rsqrt(ms + RMS_EPS) * w


def _split_bf16_k(x):
    hi = x.astype(jnp.bfloat16)
    lo = (x - hi.astype(jnp.float32)).astype(jnp.bfloat16)
    return jnp.concatenate([hi, lo], axis=1)


def _head_lanes():
    lane = lax.broadcasted_iota(jnp.int32, (1, LANES), 1)
    return lane < HEAD_DIM, lane >= HEAD_DIM


def _qkv_kernel(x_ref, meta_ref, nw_ref, w_ref, gq_ref, gk_ref, o_ref):
    i = pl.program_id(1)
    chunk = 2 * LANES
    r = lax.broadcasted_iota(jnp.int32, (2 * chunk, chunk), 0) % chunk // HEAD_DIM
    c = lax.broadcasted_iota(jnp.int32, (2 * chunk, chunk), 1) // HEAD_DIM
    head_ones = jnp.where(r == c, 1.0, 0.0).astype(jnp.bfloat16)

    def head_norm(a, gain):
        ssq = jnp.dot(_split_bf16_k(a * a), head_ones, preferred_element_type=jnp.float32)
        return a * lax.rsqrt(ssq * (1.0 / HEAD_DIM) + RMS_EPS) * gain

    def project(rows, r0, nrows):
        def emit(blk, a):
            o_ref[0, blk, r0:r0 + nrows, :] = a.astype(o_ref.dtype)

        xn = _rms_rows(rows, nw_ref[...]).astype(jnp.bfloat16)
        for j in range(QKV_PROJ_WIDTH // chunk):
            a = jnp.dot(xn, w_ref[:, j * chunk:(j + 1) * chunk], preferred_element_type=jnp.float32)
            blk = 2 * j
            if blk < BLK_KA:
                a = a * (HEAD_DIM ** -0.5)
            elif BLK_QB <= blk < BLK_KVB:
                a = head_norm(a, gq_ref[...])
            if blk < BLK_KVB:
                emit(blk, a[:, :LANES])
                emit(blk + 1, a[:, LANES:])
            else:
                kb = head_norm(a, gk_ref[...])[:, :LANES]
                vb = a[:, LANES:]
                emit(BLK_KVB, kb)
                emit(BLK_KVB + 1, pltpu.roll(kb, HEAD_DIM, 1))
                emit(BLK_KVB + 2, vb)
                emit(BLK_KVB + 3, pltpu.roll(vb, HEAD_DIM, 1))

    @pl.when(i == 0)
    def _():
        o_ref[0, :, :FIRST_VALID, :] = jnp.zeros((QKV_BLOCKS, FIRST_VALID, LANES), o_ref.dtype)
        project(meta_ref[...], FIRST_VALID, N_META)

    @pl.when(i > 0)
    def _():
        project(x_ref[0], 0, ROW_TILE)


def _qkv_project(x, meta, norm_w, w_qkv, gq, gk):
    b, seq, d = x.shape
    assert LEAD == ROW_TILE and seq % ROW_TILE == 0
    n_tiles = (seq + LEAD) // ROW_TILE
    const = lambda bi, i: (0, 0)
    return pl.pallas_call(
        _qkv_kernel,
        grid=(b, n_tiles),
        in_specs=[
            pl.BlockSpec((1, ROW_TILE, d), lambda bi, i: (bi, jnp.maximum(i - 1, 0), 0)),
            pl.BlockSpec((N_META, d), const),
            pl.BlockSpec((1, d), const),
            pl.BlockSpec((d, QKV_PROJ_WIDTH), const),
            pl.BlockSpec((1, 2 * LANES), const),
            pl.BlockSpec((1, 2 * LANES), const),
        ],
        out_specs=pl.BlockSpec((1, QKV_BLOCKS, ROW_TILE, LANES), lambda bi, i: (bi, 0, i, 0)),
        out_shape=jax.ShapeDtypeStruct((b, QKV_BLOCKS, seq + LEAD, LANES), jnp.bfloat16),
        compiler_params=pltpu.CompilerParams(
            dimension_semantics=("arbitrary", "arbitrary"), vmem_limit_bytes=VMEM_LIMIT),
        name="qkv_project",
    )(x, meta, norm_w, w_qkv, gq, gk)


def _sb_kernel(q_ref, k_ref, v_ref, o_ref, carry_ref):
    qt = pl.program_id(1) + LEAD // TILE
    head_lanes = _head_lanes()
    row = lax.broadcasted_iota(jnp.int32, (TILE, TILE), 0)
    col = lax.broadcasted_iota(jnp.int32, (TILE, TILE), 1)
    r2 = lax.broadcasted_iota(jnp.int32, (2 * TILE, TILE), 0) % TILE
    c2 = lax.broadcasted_iota(jnp.int32, (2 * TILE, TILE), 1)
    later_sum = jnp.where(r2 > c2, 1.0, 0.0).astype(jnp.bfloat16)
    masks = {"diag": row > col, "lead": col >= FIRST_VALID - TILE, "full": None}

    def tile(p, j, mode):
        mask = masks[mode]
        start = pl.multiple_of(j * TILE, TILE)
        q2 = q_ref[0, p]
        k2 = k_ref[0, p, pl.ds(start, TILE), :]
        v2 = v_ref[0, p, pl.ds(start, TILE), :]
        upd = None
        for h in range(2):
            qm = jnp.where(head_lanes[h], q2, jnp.zeros_like(q2))
            z = lax.dot_general(qm, k2, _NT, preferred_element_type=jnp.float32)
            neg_l = jnp.maximum(z, 0.0) + jnp.log(1.0 + jnp.exp(-jnp.abs(z)))
            if mask is not None:
                neg_l = jnp.where(mask, neg_l, 0.0)
            later = jnp.dot(_split_bf16_k(neg_l), later_sum, preferred_element_type=jnp.float32)
            total = later[:, :1] + neg_l[:, :1]
            logw = z - neg_l - later
            if mode != "diag":
                logw = logw - carry_ref[p, h]
            w = jnp.exp(logw)
            if mask is not None:
                w = jnp.where(mask, w, 0.0)
            vm = jnp.where(head_lanes[h], v2, jnp.zeros_like(v2))
            pv = jnp.dot(w.astype(jnp.bfloat16), vm, preferred_element_type=jnp.float32)
            upd = pv if upd is None else upd + pv
            carry_ref[p, h] = total if mode == "diag" else carry_ref[p, h] + total
        if mode == "diag":
            o_ref[0, p] = upd
        else:
            o_ref[0, p] += upd

    def all_pairs(j, mode):
        def body(p, _):
            tile(p, j, mode)
            return 0
        lax.fori_loop(0, N_PAIRS, body, 0)

    def still_visible():
        return jnp.min(carry_ref[...]) < -EXP_ZERO_BELOW

    all_pairs(qt, "diag")

    def cond(state):
        j, go = state
        return (j >= LEAD // TILE) & go

    def body(state):
        j, _ = state
        all_pairs(j, "full")
        return j - 1, still_visible()

    j_end, go = lax.while_loop(cond, body, (qt - 1, still_visible()))

    @pl.when(go)
    def _():
        all_pairs(LEAD // TILE - 1, "lead")


def _stick_breaking(qkv, b, seq):
    lp = seq + LEAD
    kv_spec = lambda blk: pl.BlockSpec((1, N_PAIRS, lp, LANES), lambda bi, i: (bi, blk // N_PAIRS, 0, 0))
    return pl.pallas_call(
        _sb_kernel,
        grid=(b, seq // TILE),
        in_specs=[
            pl.BlockSpec((1, N_PAIRS, TILE, LANES), lambda bi, i: (bi, BLK_QA // N_PAIRS, i + LEAD // TILE, 0)),
            kv_spec(BLK_KA),
            kv_spec(BLK_VA),
        ],
        out_specs=pl.BlockSpec((1, N_PAIRS, TILE, LANES), lambda bi, i: (bi, 0, i, 0)),
        out_shape=jax.ShapeDtypeStruct((b, N_PAIRS, seq, LANES), jnp.float32),
        scratch_shapes=[pltpu.VMEM((N_PAIRS, 2, TILE, 1), jnp.float32)],
        compiler_params=pltpu.CompilerParams(
            dimension_semantics=("arbitrary", "arbitrary"), vmem_limit_bytes=VMEM_LIMIT),
        name="stick_breaking",
    )(qkv, qkv, qkv)


def _t5_bucket_table():
    rel = (SW_BLOCK + np.arange(SW_BLOCK))[:, None] - np.arange(2 * SW_BLOCK)[None, :]
    n = np.maximum(rel, 0)
    max_exact = N_BUCKETS // 2
    large = max_exact + (np.log(np.maximum(n, 1) / max_exact)
                         / math.log(MAX_DISTANCE / max_exact)
                         * (N_BUCKETS - max_exact)).astype(np.int32)
    large = np.minimum(large, N_BUCKETS - 1)
    bucket = np.where(n < max_exact, n, large)
    return np.where((rel >= 0) & (rel < WINDOW), bucket, -1).astype(np.int32)


_SW_GROUP = N_HEADS // (SW_KV_WIDTH // HEAD_DIM)
_SW_STACKS = tuple(
    tuple(h for par in range(2) for h in range(N_HEADS)
          if h % 2 == par and (0 if (h // _SW_GROUP) == par else 1) == variant)
    for variant in range(2))
_SW_STACK_ROWS = len(_SW_STACKS[0]) * SW_BLOCK


def _swa_kernel(relb_ref, sink_ref, bucket_ref, q_ref, kvp_ref, kvc_ref, o_ref, bias_ref, sinkcol_ref):
    i = pl.program_id(1)

    @pl.when((pl.program_id(0) == 0) & (i == 0))
    def _():
        bucket = bucket_ref[...]
        for variant, heads in enumerate(_SW_STACKS):
            for slot, h in enumerate(heads):
                t = jnp.full(bucket.shape, -jnp.inf, jnp.float32)
                for bkt in range(N_BUCKETS):
                    t = jnp.where(bucket == bkt, relb_ref[bkt, h], t)
                bias_ref[variant, slot * SW_BLOCK:(slot + 1) * SW_BLOCK, :] = t
                sinkcol_ref[variant, slot * SW_BLOCK:(slot + 1) * SW_BLOCK, :] = jnp.full(
                    (SW_BLOCK, 1), sink_ref[h], jnp.float32)

    head_lanes = _head_lanes()
    col = lax.broadcasted_iota(jnp.int32, (1, 2 * SW_BLOCK), 1)
    lead_pen = jnp.where((i == 0) & (col < FIRST_VALID - (LEAD - SW_BLOCK)), -jnp.inf, 0.0)

    for sb in range(TILE // SW_BLOCK):
        def kv_rows(blk):
            cur = kvc_ref[0, blk, :(sb + 1) * SW_BLOCK, :]
            if sb == 0:
                return jnp.concatenate([kvp_ref[0, blk], cur], axis=0)
            return cur[(sb - 1) * SW_BLOCK:]

        out = [None] * N_PAIRS
        for variant, heads in enumerate(_SW_STACKS):
            k2 = kv_rows(variant)
            v2 = kv_rows(2 + variant)
            qs = []
            for h in heads:
                q2 = q_ref[0, h // 2, sb * SW_BLOCK:(sb + 1) * SW_BLOCK, :]
                qs.append(jnp.where(head_lanes[h % 2], q2, jnp.zeros_like(q2)))
            logits = lax.dot_general(jnp.concatenate(qs, axis=0), k2, _NT,
                                     preferred_element_type=jnp.float32)
            logits = logits + bias_ref[variant]
            if sb == 0:
                logits = logits + lead_pen
            sink = sinkcol_ref[variant]
            m = jnp.maximum(jnp.max(logits, axis=-1, keepdims=True), sink)
            e = jnp.exp(logits - m)
            denom = jnp.sum(e, axis=-1, keepdims=True) + jnp.exp(sink - m)
            pn = (e / denom).astype(jnp.bfloat16)
            half = _SW_STACK_ROWS // 2
            for par in range(2):
                vm = jnp.where(head_lanes[par], v2, jnp.zeros_like(v2))
                pv = jnp.dot(pn[par * half:(par + 1) * half], vm, preferred_element_type=jnp.float32)
                for s in range(half // SW_BLOCK):
                    h = heads[par * (half // SW_BLOCK) + s]
                    part = pv[s * SW_BLOCK:(s + 1) * SW_BLOCK]
                    out[h // 2] = part if out[h // 2] is None else out[h // 2] + part
        for p in range(N_PAIRS):
            o_ref[0, p, sb * SW_BLOCK:(sb + 1) * SW_BLOCK, :] = out[p]


def _sliding_window(qkv, rel_bias, sinks, b, seq):
    assert all(len(s) == N_HEADS // 2 for s in _SW_STACKS)
    bucket = jnp.asarray(_t5_bucket_table())
    smem = pl.BlockSpec(memory_space=pltpu.SMEM)
    sub = TILE // SW_BLOCK
    return pl.pallas_call(
        _swa_kernel,
        grid=(b, seq // TILE),
        in_specs=[
            smem, smem,
            pl.BlockSpec((SW_BLOCK, 2 * SW_BLOCK), lambda bi, i: (0, 0)),
            pl.BlockSpec((1, N_PAIRS, TILE, LANES), lambda bi, i: (bi, BLK_QB // N_PAIRS, i + LEAD // TILE, 0)),
            pl.BlockSpec((1, 4, SW_BLOCK, LANES),
                         lambda bi, i: (bi, BLK_KVB // 4, (i + LEAD // TILE) * sub - 1, 0)),
            pl.BlockSpec((1, 4, TILE, LANES), lambda bi, i: (bi, BLK_KVB // 4, i + LEAD // TILE, 0)),
        ],
        out_specs=pl.BlockSpec((1, N_PAIRS, TILE, LANES), lambda bi, i: (bi, 0, i, 0)),
        out_shape=jax.ShapeDtypeStruct((b, N_PAIRS, seq, LANES), jnp.float32),
        scratch_shapes=[pltpu.VMEM((2, _SW_STACK_ROWS, 2 * SW_BLOCK), jnp.float32),
                        pltpu.VMEM((2, _SW_STACK_ROWS, 1), jnp.float32)],
        compiler_params=pltpu.CompilerParams(
            dimension_semantics=("arbitrary", "arbitrary"), vmem_limit_bytes=VMEM_LIMIT),
        name="sliding_window",
    )(rel_bias, sinks, bucket, qkv, qkv, qkv)


def _epilogue_kernel(x_ref, oa_ref, ob_ref, nw_ref, wg_ref, wb_ref, wo_ref, o_ref):
    x = x_ref[0]
    xn = _rms_rows(x, nw_ref[...]).astype(jnp.bfloat16)
    merged = None
    for g, o_branch in enumerate((oa_ref, ob_ref)):
        z = jnp.dot(xn, wg_ref[:, g * BRANCH_WIDTH:(g + 1) * BRANCH_WIDTH],
                    preferred_element_type=jnp.float32)
        o_b = jnp.concatenate([o_branch[0, p] for p in range(N_PAIRS)], axis=1)
        u = (o_b * (z * jax.nn.sigmoid(z))).astype(jnp.bfloat16)
        y = jnp.dot(u, wb_ref[g], preferred_element_type=jnp.float32)
        g0 = 2 * BRANCH_WIDTH + g * D_MODEL
        gate = jnp.dot(xn, wg_ref[:, g0:g0 + D_MODEL], preferred_element_type=jnp.float32)
        term = jax.nn.sigmoid(gate) * y
        merged = term if merged is None else merged + term
    o_ref[0] = x + jnp.dot(merged.astype(jnp.bfloat16), wo_ref[...],
                           preferred_element_type=jnp.float32)


def _epilogue(x, oa, ob, norm_w, w_gate, w_branch, w_out):
    b, seq, d = x.shape
    tm = TILE
    const2 = lambda bi, i: (0, 0)
    branch_spec = pl.BlockSpec((1, N_PAIRS, tm, LANES), lambda bi, i: (bi, 0, i, 0))
    return pl.pallas_call(
        _epilogue_kernel,
        grid=(b, seq // tm),
        in_specs=[
            pl.BlockSpec((1, tm, d), lambda bi, i: (bi, i, 0)),
            branch_spec,
            branch_spec,
            pl.BlockSpec((1, d), const2),
            pl.BlockSpec((d, GATE_WIDTH), const2),
            pl.BlockSpec((2, BRANCH_WIDTH, d), lambda bi, i: (0, 0, 0)),
            pl.BlockSpec((d, d), const2),
        ],
        out_specs=pl.BlockSpec((1, tm, d), lambda bi, i: (bi, i, 0)),
        out_shape=jax.ShapeDtypeStruct((b, seq, d), jnp.float32),
        compiler_params=pltpu.CompilerParams(
            dimension_semantics=("arbitrary", "arbitrary"), vmem_limit_bytes=VMEM_LIMIT),
        name="epilogue",
    )(x, oa, ob, norm_w, w_gate, w_branch, w_out)


def kernel(x, meta, rel_bias, norm_w, w_in, q_gain, k_gain, sinks, w_branch, w_out):
    b, seq, d = x.shape
    assert d == D_MODEL and norm_w.shape[0] == 1, "single-layer block of width 1024 only"
    assert meta.shape == (N_META, d) and seq % ROW_TILE == 0
    bw, kvw = BRANCH_WIDTH, SW_KV_WIDTH
    w = w_in[0]
    o_qb = 4 * bw
    o_zb, o_g = o_qb + bw + 2 * kvw, o_qb + 2 * bw + 2 * kvw
    w_qkv = jnp.concatenate([w[:, :3 * bw], w[:, o_qb:o_zb]], axis=1).astype(jnp.bfloat16)
    w_gate = jnp.concatenate([w[:, 3 * bw:4 * bw], w[:, o_zb:]], axis=1).astype(jnp.bfloat16)
    gq = jnp.tile(q_gain[0], 4)[None, :] * (HEAD_DIM ** -0.5)
    gk = jnp.tile(k_gain[0], 4)[None, :]
    nw = norm_w[0][None, :]

    qkv = _qkv_project(x, meta.astype(x.dtype), nw, w_qkv, gq, gk)
    oa = _stick_breaking(qkv, b, seq)
    ob = _sliding_window(qkv, rel_bias, sinks[0], b, seq)
    return _epilogue(x, oa, ob, nw, w_gate, w_branch[0].astype(jnp.bfloat16),
                     w_out[0].astype(jnp.bfloat16))
```

```python
import math

import jax
import jax.numpy as jnp
import numpy as np
from jax import lax
from jax.experimental import pallas as pl
from jax.experimental.pallas import tpu as pltpu

D_MODEL = 1024
HEAD_DIM = 64
BRANCH_WIDTH = 512
SW_KV_WIDTH = 128
N_HEADS = BRANCH_WIDTH // HEAD_DIM
N_META = 16
WINDOW = 128
SW_BLOCK = 128
N_BUCKETS = 32
MAX_DISTANCE = 128
RMS_EPS = 1e-6

LANES = 128
TILE = 256
LEAD = 2 * TILE
FIRST_VALID = LEAD - N_META
ROW_TILE = 512
N_PAIRS = BRANCH_WIDTH // LANES
QKV_PROJ_WIDTH = 4 * BRANCH_WIDTH + 2 * SW_KV_WIDTH
QKV_BLOCKS = 4 * N_PAIRS + 4
BLK_QA, BLK_KA, BLK_VA, BLK_QB, BLK_KVB = 0, N_PAIRS, 2 * N_PAIRS, 3 * N_PAIRS, 4 * N_PAIRS
GATE_WIDTH = 2 * BRANCH_WIDTH + 2 * D_MODEL
EXP_ZERO_BELOW = -104.0
VMEM_LIMIT = 56 * 1024 * 1024

_NT = (((1,), (1,)), ((), ()))


def _rms_rows(x, w):
    ms = jnp.mean(x * x, axis=-1, keepdims=True)
    return x * lax.rsqrt(ms + RMS_EPS) * w


def _split_bf16_k(x):
    hi = x.astype(jnp.bfloat16)
    lo = (x - hi.astype(jnp.float32)).astype(jnp.bfloat16)
    return jnp.concatenate([hi, lo], axis=1)


def _head_lanes():
    lane = lax.broadcasted_iota(jnp.int32, (1, LANES), 1)
    return lane < HEAD_DIM, lane >= HEAD_DIM


def _qkv_kernel(x_ref, meta_ref, nw_ref, w_ref, gq_ref, gk_ref, o_ref):
    i = pl.program_id(1)
    chunk = 2 * LANES
    r = lax.broadcasted_iota(jnp.int32, (2 * chunk, chunk), 0) % chunk // HEAD_DIM
    c = lax.broadcasted_iota(jnp.int32, (2 * chunk, chunk), 1) // HEAD_DIM
    head_ones = jnp.where(r == c, 1.0, 0.0).astype(jnp.bfloat16)

    def head_norm(a, gain):
        ssq = jnp.dot(_split_bf16_k(a * a), head_ones, preferred_element_type=jnp.float32)
        return a * lax.rsqrt(ssq * (1.0 / HEAD_DIM) + RMS_EPS) * gain

    def project(rows, r0, nrows):
        def emit(blk, a):
            o_ref[0, blk, r0:r0 + nrows, :] = a.astype(o_ref.dtype)

        xn = _rms_rows(rows, nw_ref[...]).astype(jnp.bfloat16)
        for j in range(QKV_PROJ_WIDTH // chunk):
            a = jnp.dot(xn, w_ref[:, j * chunk:(j + 1) * chunk], preferred_element_type=jnp.float32)
            blk = 2 * j
            if blk < BLK_KA:
                a = a * (HEAD_DIM ** -0.5)
            elif BLK_QB <= blk < BLK_KVB:
                a = head_norm(a, gq_ref[...])
            if blk < BLK_KVB:
                emit(blk, a[:, :LANES])
                emit(blk + 1, a[:, LANES:])
            else:
                kb = head_norm(a, gk_ref[...])[:, :LANES]
                vb = a[:, LANES:]
                emit(BLK_KVB, kb)
                emit(BLK_KVB + 1, pltpu.roll(kb, HEAD_DIM, 1))
                emit(BLK_KVB + 2, vb)
                emit(BLK_KVB + 3, pltpu.roll(vb, HEAD_DIM, 1))

    @pl.when(i == 0)
    def _():
        o_ref[0, :, :FIRST_VALID, :] = jnp.zeros((QKV_BLOCKS, FIRST_VALID, LANES), o_ref.dtype)
        project(meta_ref[...], FIRST_VALID, N_META)

    @pl.when(i > 0)
    def _():
        project(x_ref[0], 0, ROW_TILE)


def _qkv_project(x, meta, norm_w, w_qkv, gq, gk):
    b, seq, d = x.shape
    assert LEAD == ROW_TILE and seq % ROW_TILE == 0
    n_tiles = (seq + LEAD) // ROW_TILE
    const = lambda bi, i: (0, 0)
    return pl.pallas_call(
        _qkv_kernel,
        grid=(b, n_tiles),
        in_specs=[
            pl.BlockSpec((1, ROW_TILE, d), lambda bi, i: (bi, jnp.maximum(i - 1, 0), 0)),
            pl.BlockSpec((N_META, d), const),
            pl.BlockSpec((1, d), const),
            pl.BlockSpec((d, QKV_PROJ_WIDTH), const),
            pl.BlockSpec((1, 2 * LANES), const),
            pl.BlockSpec((1, 2 * LANES), const),
        ],
        out_specs=pl.BlockSpec((1, QKV_BLOCKS, ROW_TILE, LANES), lambda bi, i: (bi, 0, i, 0)),
        out_shape=jax.ShapeDtypeStruct((b, QKV_BLOCKS, seq + LEAD, LANES), jnp.bfloat16),
        compiler_params=pltpu.CompilerParams(
            dimension_semantics=("arbitrary", "arbitrary"), vmem_limit_bytes=VMEM_LIMIT),
        name="qkv_project",
    )(x, meta, norm_w, w_qkv, gq, gk)


def _sb_kernel(q_ref, k_ref, v_ref, o_ref, carry_ref):
    qt = pl.program_id(1) + LEAD // TILE
    head_lanes = _head_lanes()
    row = lax.broadcasted_iota(jnp.int32, (TILE, TILE), 0)
    col = lax.broadcasted_iota(jnp.int32, (TILE, TILE), 1)
    r2 = lax.broadcasted_iota(jnp.int32, (2 * TILE, TILE), 0) % TILE
    c2 = lax.broadcasted_iota(jnp.int32, (2 * TILE, TILE), 1)
    later_sum = jnp.where(r2 > c2, 1.0, 0.0).astype(jnp.bfloat16)
    masks = {"diag": row > col, "lead": col >= FIRST_VALID - TILE, "full": None}

    def tile(p, j, mode):
        mask = masks[mode]
        start = pl.multiple_of(j * TILE, TILE)
        q2 = q_ref[0, p]
        k2 = k_ref[0, p, pl.ds(start, TILE), :]
        v2 = v_ref[0, p, pl.ds(start, TILE), :]
        upd = None
        for h in range(2):
            qm = jnp.where(head_lanes[h], q2, jnp.zeros_like(q2))
            z = lax.dot_general(qm, k2, _NT, preferred_element_type=jnp.float32)
            neg_l = jnp.maximum(z, 0.0) + jnp.log(1.0 + jnp.exp(-jnp.abs(z)))
            if mask is not None:
                neg_l = jnp.where(mask, neg_l, 0.0)
            later = jnp.dot(_split_bf16_k(neg_l), later_sum, preferred_element_type=jnp.float32)
            total = later[:, :1] + neg_l[:, :1]
            logw = z - neg_l - later
            if mode != "diag":
                logw = logw - carry_ref[p, h]
            w = jnp.exp(logw)
            if mask is not None:
                w = jnp.where(mask, w, 0.0)
            vm = jnp.where(head_lanes[h], v2, jnp.zeros_like(v2))
            pv = jnp.dot(w.astype(jnp.bfloat16), vm, preferred_element_type=jnp.float32)
            upd = pv if upd is None else upd + pv
            carry_ref[p, h] = total if mode == "diag" else carry_ref[p, h] + total
        if mode == "diag":
            o_ref[0, p] = upd
        else:
            o_ref[0, p] += upd

    def all_pairs(j, mode):
        for p in range(N_PAIRS):
            tile(p, j, mode)

    def still_visible():
        return jnp.min(carry_ref[...]) < -EXP_ZERO_BELOW

    all_pairs(qt, "diag")

    def cond(state):
        j, go = state
        return (j >= LEAD // TILE) & go

    def body(state):
        j, _ = state
        all_pairs(j, "full")
        return j - 1, still_visible()

    j_end, go = lax.while_loop(cond, body, (qt - 1, still_visible()))

    @pl.when(go)
    def _():
        all_pairs(LEAD // TILE - 1, "lead")


def _stick_breaking(qkv, b, seq):
    lp = seq + LEAD
    kv_spec = lambda blk: pl.BlockSpec((1, N_PAIRS, lp, LANES), lambda bi, i: (bi, blk // N_PAIRS, 0, 0))
    return pl.pallas_call(
        _sb_kernel,
        grid=(b, seq // TILE),
        in_specs=[
            pl.BlockSpec((1, N_PAIRS, TILE, LANES), lambda bi, i: (bi, BLK_QA // N_PAIRS, i + LEAD // TILE, 0)),
            kv_spec(BLK_KA),
            kv_spec(BLK_VA),
        ],
        out_specs=pl.BlockSpec((1, N_PAIRS, TILE, LANES), lambda bi, i: (bi, 0, i, 0)),
        out_shape=jax.ShapeDtypeStruct((b, N_PAIRS, seq, LANES), jnp.float32),
        scratch_shapes=[pltpu.VMEM((N_PAIRS, 2, TILE, 1), jnp.float32)],
        compiler_params=pltpu.CompilerParams(
            dimension_semantics=("arbitrary", "arbitrary"), vmem_limit_bytes=VMEM_LIMIT),
        name="stick_breaking",
    )(qkv, qkv, qkv)


def _t5_bucket_table():
    r = np.arange(SW_BLOCK)[:, None]
    l = np.arange(SW_BLOCK)[None, :]
    n = np.where(l > r, SW_BLOCK + r - l, r - l)
    assert WINDOW == SW_BLOCK and n.min() >= 0 and n.max() < WINDOW
    max_exact = N_BUCKETS // 2
    large = max_exact + (np.log(np.maximum(n, 1) / max_exact)
                         / math.log(MAX_DISTANCE / max_exact)
                         * (N_BUCKETS - max_exact)).astype(np.int32)
    large = np.minimum(large, N_BUCKETS - 1)
    return np.where(n < max_exact, n, large).astype(np.int32)


_SW_GROUP = N_HEADS // (SW_KV_WIDTH // HEAD_DIM)
_SW_STACKS = tuple(
    tuple(h for par in range(2) for h in range(N_HEADS)
          if h % 2 == par and (0 if (h // _SW_GROUP) == par else 1) == variant)
    for variant in range(2))
_SW_STACK_ROWS = len(_SW_STACKS[0]) * SW_BLOCK


def _swa_kernel(relb_ref, sink_ref, bucket_ref, q_ref, kvp_ref, kvc_ref, o_ref, bias_ref):
    i = pl.program_id(1)

    @pl.when((pl.program_id(0) == 0) & (i == 0))
    def _():
        bucket = bucket_ref[...]
        for h in range(N_HEADS):
            t = jnp.zeros(bucket.shape, jnp.float32)
            for bkt in range(N_BUCKETS):
                t = jnp.where(bucket == bkt, relb_ref[bkt, h], t)
            bias_ref[h] = t

    head_lanes = _head_lanes()
    lane = lax.broadcasted_iota(jnp.int32, (SW_BLOCK, SW_BLOCK), 1)
    from_prev = lane > lax.broadcasted_iota(jnp.int32, (SW_BLOCK, SW_BLOCK), 0)
    lane1 = lax.broadcasted_iota(jnp.int32, (1, SW_BLOCK), 1)
    lead_pen = jnp.where((i == 0) & (lane1 < FIRST_VALID - (LEAD - SW_BLOCK)), -jnp.inf, 0.0)
    ones_cols = [jnp.broadcast_to(jnp.where(m, 1.0, 0.0).astype(jnp.bfloat16), (2 * SW_BLOCK, LANES))
                 for m in head_lanes]
    n_half = len(_SW_STACKS[0]) // 2

    for sb in range(TILE // SW_BLOCK):
        rows = slice(sb * SW_BLOCK, (sb + 1) * SW_BLOCK)

        def kv_rows(blk):
            cur = kvc_ref[0, blk, :(sb + 1) * SW_BLOCK, :]
            if sb == 0:
                return jnp.concatenate([kvp_ref[0, blk], cur], axis=0)
            return cur[(sb - 1) * SW_BLOCK:]

        num = [None] * N_PAIRS
        den = [None] * N_PAIRS
        sink_term = [None] * N_HEADS
        for variant, heads in enumerate(_SW_STACKS):
            k2 = kv_rows(variant)
            v2 = kv_rows(2 + variant)
            qs = []
            for h in heads:
                q2 = q_ref[0, h // 2, rows, :]
                qs.append(jnp.where(head_lanes[h % 2], q2, jnp.zeros_like(q2)))
            logits = lax.dot_general(jnp.concatenate(qs, axis=0), k2, _NT,
                                     preferred_element_type=jnp.float32)
            weights = []
            for slot, h in enumerate(heads):
                lg = logits[slot * SW_BLOCK:(slot + 1) * SW_BLOCK]
                prev = lg[:, :SW_BLOCK]
                if sb == 0:
                    prev = prev + lead_pen
                folded = jnp.where(from_prev, prev, lg[:, SW_BLOCK:]) + bias_ref[h]
                m = jnp.maximum(jnp.max(folded, axis=-1, keepdims=True), sink_ref[h])
                m = jnp.broadcast_to(m, folded.shape)
                e = jnp.exp(folded - m)
                sink_term[h] = jnp.exp(sink_ref[h] - m)
                zero = jnp.zeros_like(e)
                weights.append(jnp.concatenate(
                    [jnp.where(from_prev, e, zero), jnp.where(from_prev, zero, e)],
                    axis=1).astype(jnp.bfloat16))
            for par in range(2):
                vm = jnp.where(head_lanes[par], v2, jnp.zeros_like(v2))
                res = jnp.dot(jnp.concatenate(weights[par * n_half:(par + 1) * n_half], axis=0),
                              jnp.concatenate([vm, ones_cols[par]], axis=1),
                              preferred_element_type=jnp.float32)
                for s in range(n_half):
                    p = heads[par * n_half + s] // 2
                    part = res[s * SW_BLOCK:(s + 1) * SW_BLOCK]
                    num[p] = part[:, :LANES] if num[p] is None else num[p] + part[:, :LANES]
                    den[p] = part[:, LANES:] if den[p] is None else den[p] + part[:, LANES:]
        for p in range(N_PAIRS):
            denom = den[p] + jnp.where(head_lanes[0], sink_term[2 * p], sink_term[2 * p + 1])
            o_ref[0, p, rows, :] = num[p] / denom


def _sliding_window(qkv, rel_bias, sinks, b, seq):
    assert all(len(s) == N_HEADS // 2 for s in _SW_STACKS)
    bucket = jnp.asarray(_t5_bucket_table())
    smem = pl.BlockSpec(memory_space=pltpu.SMEM)
    sub = TILE // SW_BLOCK
    return pl.pallas_call(
        _swa_kernel,
        grid=(b, seq // TILE),
        in_specs=[
            smem, smem,
            pl.BlockSpec((SW_BLOCK, SW_BLOCK), lambda bi, i: (0, 0)),
            pl.BlockSpec((1, N_PAIRS, TILE, LANES), lambda bi, i: (bi, BLK_QB // N_PAIRS, i + LEAD // TILE, 0)),
            pl.BlockSpec((1, 4, SW_BLOCK, LANES),
                         lambda bi, i: (bi, BLK_KVB // 4, (i + LEAD // TILE) * sub - 1, 0)),
            pl.BlockSpec((1, 4, TILE, LANES), lambda bi, i: (bi, BLK_KVB // 4, i + LEAD // TILE, 0)),
        ],
        out_specs=pl.BlockSpec((1, N_PAIRS, TILE, LANES), lambda bi, i: (bi, 0, i, 0)),
        out_shape=jax.ShapeDtypeStruct((b, N_PAIRS, seq, LANES), jnp.float32),
        scratch_shapes=[pltpu.VMEM((N_HEADS, SW_BLOCK, SW_BLOCK), jnp.float32)],
        compiler_params=pltpu.CompilerParams(
            dimension_semantics=("arbitrary", "arbitrary"), vmem_limit_bytes=VMEM_LIMIT),
        name="sliding_window",
    )(rel_bias, sinks, bucket, qkv, qkv, qkv)


def _epilogue_kernel(x_ref, oa_ref, ob_ref, nw_ref, wg_ref, wb_ref, wo_ref, o_ref):
    x = x_ref[0]
    xn = _rms_rows(x, nw_ref[...]).astype(jnp.bfloat16)
    merged = None
    for g, o_branch in enumerate((oa_ref, ob_ref)):
        z = jnp.dot(xn, wg_ref[:, g * BRANCH_WIDTH:(g + 1) * BRANCH_WIDTH],
                    preferred_element_type=jnp.float32)
        o_b = jnp.concatenate([o_branch[0, p] for p in range(N_PAIRS)], axis=1)
        u = (o_b * (z * jax.nn.sigmoid(z))).astype(jnp.bfloat16)
        y = jnp.dot(u, wb_ref[g], preferred_element_type=jnp.float32)
        g0 = 2 * BRANCH_WIDTH + g * D_MODEL
        gate = jnp.dot(xn, wg_ref[:, g0:g0 + D_MODEL], preferred_element_type=jnp.float32)
        term = jax.nn.sigmoid(gate) * y
        merged = term if merged is None else merged + term
    o_ref[0] = x + jnp.dot(merged.astype(jnp.bfloat16), wo_ref[...],
                           preferred_element_type=jnp.float32)


def _epilogue(x, oa, ob, norm_w, w_gate, w_branch, w_out):
    b, seq, d = x.shape
    tm = TILE
    const2 = lambda bi, i: (0, 0)
    branch_spec = pl.BlockSpec((1, N_PAIRS, tm, LANES), lambda bi, i: (bi, 0, i, 0))
    return pl.pallas_call(
        _epilogue_kernel,
        grid=(b, seq // tm),
        in_specs=[
            pl.BlockSpec((1, tm, d), lambda bi, i: (bi, i, 0)),
            branch_spec,
            branch_spec,
            pl.BlockSpec((1, d), const2),
            pl.BlockSpec((d, GATE_WIDTH), const2),
            pl.BlockSpec((2, BRANCH_WIDTH, d), lambda bi, i: (0, 0, 0)),
            pl.BlockSpec((d, d), const2),
        ],
        out_specs=pl.BlockSpec((1, tm, d), lambda bi, i: (bi, i, 0)),
        out_shape=jax.ShapeDtypeStruct((b, seq, d), jnp.float32),
        compiler_params=pltpu.CompilerParams(
            dimension_semantics=("arbitrary", "arbitrary"), vmem_limit_bytes=VMEM_LIMIT),
        name="epilogue",
    )(x, oa, ob, norm_w, w_gate, w_branch, w_out)


def kernel(x, meta, rel_bias, norm_w, w_in, q_gain, k_gain, sinks, w_branch, w_out):
    b, seq, d = x.shape
    assert d == D_MODEL and norm_w.shape[0] == 1, "single-layer block of width 1024 only"
    assert meta.shape == (N_META, d) and seq % ROW_TILE == 0
    bw, kvw = BRANCH_WIDTH, SW_KV_WIDTH
    w = w_in[0]
    o_qb = 4 * bw
    o_zb, o_g = o_qb + bw + 2 * kvw, o_qb + 2 * bw + 2 * kvw
    w_qkv = jnp.concatenate([w[:, :3 * bw], w[:, o_qb:o_zb]], axis=1).astype(jnp.bfloat16)
    w_gate = jnp.concatenate([w[:, 3 * bw:4 * bw], w[:, o_zb:]], axis=1).astype(jnp.bfloat16)
    gq = jnp.tile(q_gain[0], 4)[None, :] * (HEAD_DIM ** -0.5)
    gk = jnp.tile(k_gain[0], 4)[None, :]
    nw = norm_w[0][None, :]

    qkv = _qkv_project(x, meta.astype(x.dtype), nw, w_qkv, gq, gk)
    oa = _stick_breaking(qkv, b, seq)
    ob = _sliding_window(qkv, rel_bias, sinks[0], b, seq)
    return _epilogue(x, oa, ob, nw, w_gate, w_branch[0].astype(jnp.bfloat16),
                     w_out[0].astype(jnp.bfloat16))
```

```python
import math

import jax
import jax.numpy as jnp
import numpy as np
from jax import lax
from jax.experimental import pallas as pl
from jax.experimental.pallas import tpu as pltpu

D_MODEL = 1024
HEAD_DIM = 64
BRANCH_WIDTH = 512
SW_KV_WIDTH = 128
N_HEADS = BRANCH_WIDTH // HEAD_DIM
N_META = 16
WINDOW = 128
SW_BLOCK = 128
N_BUCKETS = 32
MAX_DISTANCE = 128
RMS_EPS = 1e-6

LANES = 128
TILE = 256
LEAD = 2 * TILE
FIRST_VALID = LEAD - N_META
ROW_TILE = 512
N_PAIRS = BRANCH_WIDTH // LANES
QKV_PROJ_WIDTH = 4 * BRANCH_WIDTH + 2 * SW_KV_WIDTH
QKV_BLOCKS = 4 * N_PAIRS + 4
BLK_QA, BLK_KA, BLK_VA, BLK_QB, BLK_KVB = 0, N_PAIRS, 2 * N_PAIRS, 3 * N_PAIRS, 4 * N_PAIRS
GATE_WIDTH = 2 * BRANCH_WIDTH + 2 * D_MODEL
EXP_ZERO_BELOW = -105.0
VMEM_LIMIT = 56 * 1024 * 1024

_NT = (((1,), (1,)), ((), ()))


def _rms_rows(x, w):
    ms = jnp.mean(x * x, axis=-1, keepdims=True)
    return x * lax.rsqrt(ms + RMS_EPS) * w


def _neg_abs(x):
    bits = pltpu.bitcast(x, jnp.uint32) | jnp.uint32(0x80000000)
    return pltpu.bitcast(bits, jnp.float32)


def _head_lanes():
    lane = lax.broadcasted_iota(jnp.int32, (1, LANES), 1)
    return lane < HEAD_DIM, lane >= HEAD_DIM


def _qkv_kernel(x_ref, meta_ref, nw_ref, w_ref, gq_ref, gk_ref, o_ref):
    i = pl.program_id(1)
    chunk = 2 * LANES
    r = lax.broadcasted_iota(jnp.int32, (chunk, chunk), 0) // HEAD_DIM
    c = lax.broadcasted_iota(jnp.int32, (chunk, chunk), 1) // HEAD_DIM
    head_ones = jnp.where(r == c, 1.0, 0.0).astype(jnp.bfloat16)

    def head_norm(a, gain):
        ssq = jnp.dot((a * a).astype(jnp.bfloat16), head_ones, preferred_element_type=jnp.float32)
        return a * lax.rsqrt(ssq * (1.0 / HEAD_DIM) + RMS_EPS) * gain

    def project(rows, r0, nrows):
        def emit(blk, a):
            o_ref[0, blk, r0:r0 + nrows, :] = a.astype(o_ref.dtype)

        xn = _rms_rows(rows, nw_ref[...]).astype(jnp.bfloat16)
        for j in range(QKV_PROJ_WIDTH // chunk):
            a = jnp.dot(xn, w_ref[:, j * chunk:(j + 1) * chunk], preferred_element_type=jnp.float32)
            blk = 2 * j
            if blk < BLK_KA:
                a = a * (HEAD_DIM ** -0.5)
            elif BLK_QB <= blk < BLK_KVB:
                a = head_norm(a, gq_ref[...])
            if blk < BLK_KVB:
                emit(blk, a[:, :LANES])
                emit(blk + 1, a[:, LANES:])
            else:
                kb = head_norm(a, gk_ref[...])[:, :LANES]
                vb = a[:, LANES:]
                emit(BLK_KVB, kb)
                emit(BLK_KVB + 1, pltpu.roll(kb, HEAD_DIM, 1))
                emit(BLK_KVB + 2, vb)
                emit(BLK_KVB + 3, pltpu.roll(vb, HEAD_DIM, 1))

    @pl.when(i == 0)
    def _():
        o_ref[0, :, :FIRST_VALID, :] = jnp.zeros((QKV_BLOCKS, FIRST_VALID, LANES), o_ref.dtype)
        project(meta_ref[...], FIRST_VALID, N_META)

    @pl.when(i > 0)
    def _():
        project(x_ref[0], 0, ROW_TILE)


def _qkv_project(x, meta, norm_w, w_qkv, gq, gk):
    b, seq, d = x.shape
    assert LEAD == ROW_TILE and seq % ROW_TILE == 0
    n_tiles = (seq + LEAD) // ROW_TILE
    const = lambda bi, i: (0, 0)
    return pl.pallas_call(
        _qkv_kernel,
        grid=(b, n_tiles),
        in_specs=[
            pl.BlockSpec((1, ROW_TILE, d), lambda bi, i: (bi, jnp.maximum(i - 1, 0), 0)),
            pl.BlockSpec((N_META, d), const),
            pl.BlockSpec((1, d), const),
            pl.BlockSpec((d, QKV_PROJ_WIDTH), const),
            pl.BlockSpec((1, 2 * LANES), const),
            pl.BlockSpec((1, 2 * LANES), const),
        ],
        out_specs=pl.BlockSpec((1, QKV_BLOCKS, ROW_TILE, LANES), lambda bi, i: (bi, 0, i, 0)),
        out_shape=jax.ShapeDtypeStruct((b, QKV_BLOCKS, seq + LEAD, LANES), jnp.bfloat16),
        compiler_params=pltpu.CompilerParams(
            dimension_semantics=("arbitrary", "arbitrary"), vmem_limit_bytes=VMEM_LIMIT),
        name="qkv_project",
    )(x, meta, norm_w, w_qkv, gq, gk)


def _sb_kernel(q_ref, k_ref, v_ref, o_ref, carry_ref, negl_ref, logsig_ref, first_ref):
    qt = pl.program_id(1) + LEAD // TILE
    head_lanes = _head_lanes()
    row = lax.broadcasted_iota(jnp.int32, (TILE, TILE), 0)
    col = lax.broadcasted_iota(jnp.int32, (TILE, TILE), 1)
    later_sum = jnp.where(row > col, 1.0, 0.0).astype(jnp.bfloat16)
    masks = {"diag": row > col, "lead": col >= FIRST_VALID - TILE, "full": None}

    def all_pairs(j, mode):
        mask = masks[mode]
        start = pl.multiple_of(j * TILE, TILE)
        for p in range(N_PAIRS):
            q2 = q_ref[0, p]
            k2 = k_ref[0, p, pl.ds(start, TILE), :]
            for h in range(2):
                qm = jnp.where(head_lanes[h], q2, jnp.zeros_like(q2))
                z = lax.dot_general(qm, k2, _NT, preferred_element_type=jnp.float32)
                neg_l = jnp.maximum(z, 0.0) + jnp.log(1.0 + jnp.exp(_neg_abs(z)))
                if mask is not None:
                    neg_l = jnp.where(mask, neg_l, 0.0)
                negl_ref[p, h] = neg_l.astype(jnp.bfloat16)
                logsig_ref[p, h] = z - neg_l
                first_ref[p, h] = neg_l[:, :LANES]
        for p in range(N_PAIRS):
            v2 = v_ref[0, p, pl.ds(start, TILE), :]
            upd = None
            for h in range(2):
                later = jnp.dot(negl_ref[p, h], later_sum, preferred_element_type=jnp.float32)
                total = jnp.broadcast_to(later[:, :1] + first_ref[p, h][:, :1], (TILE, LANES))
                logw = logsig_ref[p, h] - later
                if mode != "diag":
                    carry = carry_ref[p, h]
                    logw = logw - jnp.concatenate([carry] * (TILE // LANES), axis=1)
                    carry_ref[p, h] = carry + total
                else:
                    carry_ref[p, h] = total
                w = jnp.exp(logw)
                if mask is not None:
                    w = jnp.where(mask, w, 0.0)
                vm = jnp.where(head_lanes[h], v2, jnp.zeros_like(v2))
                pv = jnp.dot(w.astype(jnp.bfloat16), vm, preferred_element_type=jnp.float32)
                upd = pv if upd is None else upd + pv
            if mode == "diag":
                o_ref[0, p] = upd
            else:
                o_ref[0, p] += upd

    def still_visible():
        return jnp.min(carry_ref[...]) < -EXP_ZERO_BELOW

    all_pairs(qt, "diag")

    def cond(state):
        j, go = state
        return (j >= LEAD // TILE) & go

    def body(state):
        j, _ = state
        all_pairs(j, "full")
        return j - 1, still_visible()

    j_end, go = lax.while_loop(cond, body, (qt - 1, still_visible()))

    @pl.when(go)
    def _():
        all_pairs(LEAD // TILE - 1, "lead")


def _stick_breaking(qkv, b, seq):
    lp = seq + LEAD
    kv_spec = lambda blk: pl.BlockSpec((1, N_PAIRS, lp, LANES), lambda bi, i: (bi, blk // N_PAIRS, 0, 0))
    return pl.pallas_call(
        _sb_kernel,
        grid=(b, seq // TILE),
        in_specs=[
            pl.BlockSpec((1, N_PAIRS, TILE, LANES), lambda bi, i: (bi, BLK_QA // N_PAIRS, i + LEAD // TILE, 0)),
            kv_spec(BLK_KA),
            kv_spec(BLK_VA),
        ],
        out_specs=pl.BlockSpec((1, N_PAIRS, TILE, LANES), lambda bi, i: (bi, 0, i, 0)),
        out_shape=jax.ShapeDtypeStruct((b, N_PAIRS, seq, LANES), jnp.float32),
        scratch_shapes=[pltpu.VMEM((N_PAIRS, 2, TILE, LANES), jnp.float32),
                        pltpu.VMEM((N_PAIRS, 2, TILE, TILE), jnp.bfloat16),
                        pltpu.VMEM((N_PAIRS, 2, TILE, TILE), jnp.float32),
                        pltpu.VMEM((N_PAIRS, 2, TILE, LANES), jnp.float32)],
        compiler_params=pltpu.CompilerParams(
            dimension_semantics=("arbitrary", "arbitrary"), vmem_limit_bytes=VMEM_LIMIT),
        name="stick_breaking",
    )(qkv, qkv, qkv)


def _t5_bucket_table():
    r = np.arange(SW_BLOCK)[:, None]
    l = np.arange(SW_BLOCK)[None, :]
    n = np.where(l > r, SW_BLOCK + r - l, r - l)
    assert WINDOW == SW_BLOCK and n.min() >= 0 and n.max() < WINDOW
    max_exact = N_BUCKETS // 2
    large = max_exact + (np.log(np.maximum(n, 1) / max_exact)
                         / math.log(MAX_DISTANCE / max_exact)
                         * (N_BUCKETS - max_exact)).astype(np.int32)
    large = np.minimum(large, N_BUCKETS - 1)
    return np.where(n < max_exact, n, large).astype(np.int32)


_SW_GROUP = N_HEADS // (SW_KV_WIDTH // HEAD_DIM)
_SW_STACKS = tuple(
    tuple(h for par in range(2) for h in range(N_HEADS)
          if h % 2 == par and (0 if (h // _SW_GROUP) == par else 1) == variant)
    for variant in range(2))
_SW_STACK_ROWS = len(_SW_STACKS[0]) * SW_BLOCK


def _swa_kernel(relb_ref, sink_ref, bucket_ref, q_ref, kvp_ref, kvc_ref, o_ref, bias_ref):
    i = pl.program_id(1)

    @pl.when((pl.program_id(0) == 0) & (i == 0))
    def _():
        bucket = bucket_ref[...]
        for h in range(N_HEADS):
            t = jnp.zeros(bucket.shape, jnp.float32)
            for bkt in range(N_BUCKETS):
                t = jnp.where(bucket == bkt, relb_ref[bkt, h], t)
            bias_ref[h] = t

    head_lanes = _head_lanes()
    lane = lax.broadcasted_iota(jnp.int32, (SW_BLOCK, SW_BLOCK), 1)
    from_prev = lane > lax.broadcasted_iota(jnp.int32, (SW_BLOCK, SW_BLOCK), 0)
    lane1 = lax.broadcasted_iota(jnp.int32, (1, SW_BLOCK), 1)
    lead_pen = jnp.where((i == 0) & (lane1 < FIRST_VALID - (LEAD - SW_BLOCK)), -jnp.inf, 0.0)
    ones_cols = [jnp.broadcast_to(jnp.where(m, 1.0, 0.0).astype(jnp.bfloat16), (2 * SW_BLOCK, LANES))
                 for m in head_lanes]
    n_half = len(_SW_STACKS[0]) // 2

    for sb in range(TILE // SW_BLOCK):
        rows = slice(sb * SW_BLOCK, (sb + 1) * SW_BLOCK)

        def kv_rows(blk):
            cur = kvc_ref[0, blk, :(sb + 1) * SW_BLOCK, :]
            if sb == 0:
                return jnp.concatenate([kvp_ref[0, blk], cur], axis=0)
            return cur[(sb - 1) * SW_BLOCK:]

        num = [None] * N_PAIRS
        den = [None] * N_PAIRS
        sink_term = [None] * N_HEADS
        for variant, heads in enumerate(_SW_STACKS):
            k2 = kv_rows(variant)
            v2 = kv_rows(2 + variant)
            qs = []
            for h in heads:
                q2 = q_ref[0, h // 2, rows, :]
                qs.append(jnp.where(head_lanes[h % 2], q2, jnp.zeros_like(q2)))
            logits = lax.dot_general(jnp.concatenate(qs, axis=0), k2, _NT,
                                     preferred_element_type=jnp.float32)
            weights = []
            for slot, h in enumerate(heads):
                lg = logits[slot * SW_BLOCK:(slot + 1) * SW_BLOCK]
                prev = lg[:, :SW_BLOCK]
                if sb == 0:
                    prev = prev + lead_pen
                folded = jnp.where(from_prev, prev, lg[:, SW_BLOCK:]) + bias_ref[h]
                m = jnp.maximum(jnp.max(folded, axis=-1, keepdims=True), sink_ref[h])
                m = jnp.broadcast_to(m, folded.shape)
                e = jnp.exp(folded - m)
                sink_term[h] = jnp.exp(sink_ref[h] - m)
                zero = jnp.zeros_like(e)
                weights.append(jnp.concatenate(
                    [jnp.where(from_prev, e, zero), jnp.where(from_prev, zero, e)],
                    axis=1).astype(jnp.bfloat16))
            for par in range(2):
                vm = jnp.where(head_lanes[par], v2, jnp.zeros_like(v2))
                res = jnp.dot(jnp.concatenate(weights[par * n_half:(par + 1) * n_half], axis=0),
                              jnp.concatenate([vm, ones_cols[par]], axis=1),
                              preferred_element_type=jnp.float32)
                for s in range(n_half):
                    p = heads[par * n_half + s] // 2
                    part = res[s * SW_BLOCK:(s + 1) * SW_BLOCK]
                    num[p] = part[:, :LANES] if num[p] is None else num[p] + part[:, :LANES]
                    den[p] = part[:, LANES:] if den[p] is None else den[p] + part[:, LANES:]
        for p in range(N_PAIRS):
            denom = den[p] + jnp.where(head_lanes[0], sink_term[2 * p], sink_term[2 * p + 1])
            o_ref[0, p, rows, :] = num[p] / denom


def _sliding_window(qkv, rel_bias, sinks, b, seq):
    assert all(len(s) == N_HEADS // 2 for s in _SW_STACKS)
    bucket = jnp.asarray(_t5_bucket_table())
    smem = pl.BlockSpec(memory_space=pltpu.SMEM)
    sub = TILE // SW_BLOCK
    return pl.pallas_call(
        _swa_kernel,
        grid=(b, seq // TILE),
        in_specs=[
            smem, smem,
            pl.BlockSpec((SW_BLOCK, SW_BLOCK), lambda bi, i: (0, 0)),
            pl.BlockSpec((1, N_PAIRS, TILE, LANES), lambda bi, i: (bi, BLK_QB // N_PAIRS, i + LEAD // TILE, 0)),
            pl.BlockSpec((1, 4, SW_BLOCK, LANES),
                         lambda bi, i: (bi, BLK_KVB // 4, (i + LEAD // TILE) * sub - 1, 0)),
            pl.BlockSpec((1, 4, TILE, LANES), lambda bi, i: (bi, BLK_KVB // 4, i + LEAD // TILE, 0)),
        ],
        out_specs=pl.BlockSpec((1, N_PAIRS, TILE, LANES), lambda bi, i: (bi, 0, i, 0)),
        out_shape=jax.ShapeDtypeStruct((b, N_PAIRS, seq, LANES), jnp.float32),
        scratch_shapes=[pltpu.VMEM((N_HEADS, SW_BLOCK, SW_BLOCK), jnp.float32)],
        compiler_params=pltpu.CompilerParams(
            dimension_semantics=("arbitrary", "arbitrary"), vmem_limit_bytes=VMEM_LIMIT),
        name="sliding_window",
    )(rel_bias, sinks, bucket, qkv, qkv, qkv)


def _epilogue_kernel(x_ref, oa_ref, ob_ref, nw_ref, wg_ref, wb_ref, wo_ref, o_ref):
    x = x_ref[0]
    xn = _rms_rows(x, nw_ref[...]).astype(jnp.bfloat16)
    merged = None
    for g, o_branch in enumerate((oa_ref, ob_ref)):
        z = jnp.dot(xn, wg_ref[:, g * BRANCH_WIDTH:(g + 1) * BRANCH_WIDTH],
                    preferred_element_type=jnp.float32)
        o_b = jnp.concatenate([o_branch[0, p] for p in range(N_PAIRS)], axis=1)
        u = (o_b * (z * jax.nn.sigmoid(z))).astype(jnp.bfloat16)
        y = jnp.dot(u, wb_ref[g], preferred_element_type=jnp.float32)
        g0 = 2 * BRANCH_WIDTH + g * D_MODEL
        gate = jnp.dot(xn, wg_ref[:, g0:g0 + D_MODEL], preferred_element_type=jnp.float32)
        term = jax.nn.sigmoid(gate) * y
        merged = term if merged is None else merged + term
    o_ref[0] = x + jnp.dot(merged.astype(jnp.bfloat16), wo_ref[...],
                           preferred_element_type=jnp.float32)


def _epilogue(x, oa, ob, norm_w, w_gate, w_branch, w_out):
    b, seq, d = x.shape
    tm = ROW_TILE
    const2 = lambda bi, i: (0, 0)
    branch_spec = pl.BlockSpec((1, N_PAIRS, tm, LANES), lambda bi, i: (bi, 0, i, 0))
    return pl.pallas_call(
        _epilogue_kernel,
        grid=(b, seq // tm),
        in_specs=[
            pl.BlockSpec((1, tm, d), lambda bi, i: (bi, i, 0)),
            branch_spec,
            branch_spec,
            pl.BlockSpec((1, d), const2),
            pl.BlockSpec((d, GATE_WIDTH), const2),
            pl.BlockSpec((2, BRANCH_WIDTH, d), lambda bi, i: (0, 0, 0)),
            pl.BlockSpec((d, d), const2),
        ],
        out_specs=pl.BlockSpec((1, tm, d), lambda bi, i: (bi, i, 0)),
        out_shape=jax.ShapeDtypeStruct((b, seq, d), jnp.float32),
        compiler_params=pltpu.CompilerParams(
            dimension_semantics=("arbitrary", "arbitrary"), vmem_limit_bytes=VMEM_LIMIT),
        name="epilogue",
    )(x, oa, ob, norm_w, w_gate, w_branch, w_out)


def kernel(x, meta, rel_bias, norm_w, w_in, q_gain, k_gain, sinks, w_branch, w_out):
    b, seq, d = x.shape
    assert d == D_MODEL and norm_w.shape[0] == 1, "single-layer block of width 1024 only"
    assert meta.shape == (N_META, d) and seq % ROW_TILE == 0
    bw, kvw = BRANCH_WIDTH, SW_KV_WIDTH
    w = w_in[0]
    o_qb = 4 * bw
    o_zb, o_g = o_qb + bw + 2 * kvw, o_qb + 2 * bw + 2 * kvw
    w_qkv = jnp.concatenate([w[:, :3 * bw], w[:, o_qb:o_zb]], axis=1).astype(jnp.bfloat16)
    w_gate = jnp.concatenate([w[:, 3 * bw:4 * bw], w[:, o_zb:]], axis=1).astype(jnp.bfloat16)
    gq = jnp.tile(q_gain[0], 4)[None, :] * (HEAD_DIM ** -0.5)
    gk = jnp.tile(k_gain[0], 4)[None, :]
    nw = norm_w[0][None, :]

    qkv = _qkv_project(x, meta.astype(x.dtype), nw, w_qkv, gq, gk)
    oa = _stick_breaking(qkv, b, seq)
    ob = _sliding_window(qkv, rel_bias, sinks[0], b, seq)
    return _epilogue(x, oa, ob, nw, w_gate, w_branch[0].astype(jnp.bfloat16),
                     w_out[0].astype(jnp.bfloat16))
```

```python
import math

import jax
import jax.numpy as jnp
import numpy as np
from jax import lax
from jax.experimental import pallas as pl
from jax.experimental.pallas import tpu as pltpu

D_MODEL = 1024
HEAD_DIM = 64
BRANCH_WIDTH = 512
SW_KV_WIDTH = 128
N_HEADS = BRANCH_WIDTH // HEAD_DIM
N_META = 16
WINDOW = 128
SW_BLOCK = 128
N_BUCKETS = 32
MAX_DISTANCE = 128
RMS_EPS = 1e-6

LANES = 128
TILE = 256
LEAD = 2 * TILE
FIRST_VALID = LEAD - N_META
ROW_TILE = 512
N_PAIRS = BRANCH_WIDTH // LANES
QKV_PROJ_WIDTH = 4 * BRANCH_WIDTH + 2 * SW_KV_WIDTH
QKV_BLOCKS = 4 * N_PAIRS + 4
BLK_QA, BLK_KA, BLK_VA, BLK_QB, BLK_KVB = 0, N_PAIRS, 2 * N_PAIRS, 3 * N_PAIRS, 4 * N_PAIRS
GATE_WIDTH = 2 * BRANCH_WIDTH + 2 * D_MODEL
EXP_ZERO_BELOW = -105.0
VMEM_LIMIT = 56 * 1024 * 1024

_NT = (((1,), (1,)), ((), ()))


def _rms_rows(x, w):
    ms = jnp.mean(x * x, axis=-1, keepdims=True)
    return x * lax.rsqrt(ms + RMS_EPS) * w


def _neg_abs(x):
    bits = pltpu.bitcast(x, jnp.uint32) | jnp.uint32(0x80000000)
    return pltpu.bitcast(bits, jnp.float32)


def _head_lanes():
    lane = lax.broadcasted_iota(jnp.int32, (1, LANES), 1)
    return lane < HEAD_DIM, lane >= HEAD_DIM


def _qkv_kernel(x_ref, meta_ref, nw_ref, w_ref, gq_ref, gk_ref, o_ref):
    i = pl.program_id(1)
    chunk = 2 * LANES
    r = lax.broadcasted_iota(jnp.int32, (chunk, chunk), 0) // HEAD_DIM
    c = lax.broadcasted_iota(jnp.int32, (chunk, chunk), 1) // HEAD_DIM
    head_ones = jnp.where(r == c, 1.0, 0.0).astype(jnp.bfloat16)

    def head_norm(a, gain):
        ssq = jnp.dot((a * a).astype(jnp.bfloat16), head_ones, preferred_element_type=jnp.float32)
        return a * lax.rsqrt(ssq * (1.0 / HEAD_DIM) + RMS_EPS) * gain

    def project(rows, r0, nrows):
        def emit(blk, a):
            o_ref[0, blk, r0:r0 + nrows, :] = a.astype(o_ref.dtype)

        xn = _rms_rows(rows, nw_ref[...]).astype(jnp.bfloat16)
        n_chunks = QKV_PROJ_WIDTH // chunk
        first_norm = BLK_QB // 2
        for j in list(range(first_norm, n_chunks)) + list(range(first_norm)):
            a = jnp.dot(xn, w_ref[:, j * chunk:(j + 1) * chunk], preferred_element_type=jnp.float32)
            blk = 2 * j
            if blk < BLK_KA:
                a = a * (HEAD_DIM ** -0.5)
            elif BLK_QB <= blk < BLK_KVB:
                a = head_norm(a, gq_ref[...])
            if blk < BLK_KVB:
                emit(blk, a[:, :LANES])
                emit(blk + 1, a[:, LANES:])
            else:
                kb = head_norm(a, gk_ref[...])[:, :LANES]
                vb = a[:, LANES:]
                emit(BLK_KVB, kb)
                emit(BLK_KVB + 1, pltpu.roll(kb, HEAD_DIM, 1))
                emit(BLK_KVB + 2, vb)
                emit(BLK_KVB + 3, pltpu.roll(vb, HEAD_DIM, 1))

    @pl.when(i == 0)
    def _():
        o_ref[0, :, :FIRST_VALID, :] = jnp.zeros((QKV_BLOCKS, FIRST_VALID, LANES), o_ref.dtype)
        project(meta_ref[...], FIRST_VALID, N_META)

    @pl.when(i > 0)
    def _():
        project(x_ref[0], 0, ROW_TILE)


def _qkv_project(x, meta, norm_w, w_qkv, gq, gk):
    b, seq, d = x.shape
    assert LEAD == ROW_TILE and seq % ROW_TILE == 0
    n_tiles = (seq + LEAD) // ROW_TILE
    const = lambda bi, i: (0, 0)
    return pl.pallas_call(
        _qkv_kernel,
        grid=(b, n_tiles),
        in_specs=[
            pl.BlockSpec((1, ROW_TILE, d), lambda bi, i: (bi, jnp.maximum(i - 1, 0), 0)),
            pl.BlockSpec((N_META, d), const),
            pl.BlockSpec((1, d), const),
            pl.BlockSpec((d, QKV_PROJ_WIDTH), const),
            pl.BlockSpec((1, 2 * LANES), const),
            pl.BlockSpec((1, 2 * LANES), const),
        ],
        out_specs=pl.BlockSpec((1, QKV_BLOCKS, ROW_TILE, LANES), lambda bi, i: (bi, 0, i, 0)),
        out_shape=jax.ShapeDtypeStruct((b, QKV_BLOCKS, seq + LEAD, LANES), jnp.bfloat16),
        compiler_params=pltpu.CompilerParams(
            dimension_semantics=("arbitrary", "arbitrary"), vmem_limit_bytes=VMEM_LIMIT),
        name="qkv_project",
    )(x, meta, norm_w, w_qkv, gq, gk)


def _sb_kernel(q_ref, k_ref, v_ref, o_ref, carry_ref, negl_ref, logsig_ref, first_ref):
    qt = pl.program_id(1) + LEAD // TILE
    head_lanes = _head_lanes()
    row = lax.broadcasted_iota(jnp.int32, (TILE, TILE), 0)
    col = lax.broadcasted_iota(jnp.int32, (TILE, TILE), 1)
    later_sum = jnp.where(row > col, 1.0, 0.0).astype(jnp.bfloat16)
    masks = {"diag": row > col, "lead": col >= FIRST_VALID - TILE, "full": None}

    def all_pairs(j, mode):
        mask = masks[mode]
        start = pl.multiple_of(j * TILE, TILE)
        for p in range(N_PAIRS):
            q2 = q_ref[0, p]
            k2 = k_ref[0, p, pl.ds(start, TILE), :]
            for h in range(2):
                qm = jnp.where(head_lanes[h], q2, jnp.zeros_like(q2))
                z = lax.dot_general(qm, k2, _NT, preferred_element_type=jnp.float32)
                neg_l = jnp.maximum(z, 0.0) + jnp.log(1.0 + jnp.exp(_neg_abs(z)))
                if mask is not None:
                    neg_l = jnp.where(mask, neg_l, 0.0)
                negl_ref[p, h] = neg_l.astype(jnp.bfloat16)
                logsig_ref[p, h] = z - neg_l
                first_ref[p, h] = neg_l[:, :LANES]
        for p in range(N_PAIRS):
            v2 = v_ref[0, p, pl.ds(start, TILE), :]
            upd = None
            for h in range(2):
                later = jnp.dot(negl_ref[p, h], later_sum, preferred_element_type=jnp.float32)
                total = jnp.broadcast_to(later[:, :1] + first_ref[p, h][:, :1], (TILE, LANES))
                logw = logsig_ref[p, h] - later
                if mode != "diag":
                    carry = carry_ref[p, h]
                    logw = logw - jnp.concatenate([carry] * (TILE // LANES), axis=1)
                    carry_ref[p, h] = carry + total
                else:
                    carry_ref[p, h] = total
                w = jnp.exp(logw)
                if mask is not None:
                    w = jnp.where(mask, w, 0.0)
                vm = jnp.where(head_lanes[h], v2, jnp.zeros_like(v2))
                pv = jnp.dot(w.astype(jnp.bfloat16), vm, preferred_element_type=jnp.float32)
                upd = pv if upd is None else upd + pv
            if mode == "diag":
                o_ref[0, p] = upd
            else:
                o_ref[0, p] += upd

    def still_visible():
        return jnp.min(carry_ref[...]) < -EXP_ZERO_BELOW

    all_pairs(qt, "diag")

    def cond(state):
        j, go = state
        return (j >= LEAD // TILE) & go

    def body(state):
        j, _ = state
        all_pairs(j, "full")
        return j - 1, still_visible()

    j_end, go = lax.while_loop(cond, body, (qt - 1, still_visible()))

    @pl.when(go)
    def _():
        all_pairs(LEAD // TILE - 1, "lead")


def _stick_breaking(qkv, b, seq):
    lp = seq + LEAD
    kv_spec = lambda blk: pl.BlockSpec((1, N_PAIRS, lp, LANES), lambda bi, i: (bi, blk // N_PAIRS, 0, 0))
    return pl.pallas_call(
        _sb_kernel,
        grid=(b, seq // TILE),
        in_specs=[
            pl.BlockSpec((1, N_PAIRS, TILE, LANES), lambda bi, i: (bi, BLK_QA // N_PAIRS, i + LEAD // TILE, 0)),
            kv_spec(BLK_KA),
            kv_spec(BLK_VA),
        ],
        out_specs=pl.BlockSpec((1, N_PAIRS, TILE, LANES), lambda bi, i: (bi, 0, i, 0)),
        out_shape=jax.ShapeDtypeStruct((b, N_PAIRS, seq, LANES), jnp.float32),
        scratch_shapes=[pltpu.VMEM((N_PAIRS, 2, TILE, LANES), jnp.float32),
                        pltpu.VMEM((N_PAIRS, 2, TILE, TILE), jnp.bfloat16),
                        pltpu.VMEM((N_PAIRS, 2, TILE, TILE), jnp.float32),
                        pltpu.VMEM((N_PAIRS, 2, TILE, LANES), jnp.float32)],
        compiler_params=pltpu.CompilerParams(
            dimension_semantics=("arbitrary", "arbitrary"), vmem_limit_bytes=VMEM_LIMIT),
        name="stick_breaking",
    )(qkv, qkv, qkv)


def _t5_bucket_table():
    r = np.arange(SW_BLOCK)[:, None]
    l = np.arange(SW_BLOCK)[None, :]
    n = np.where(l > r, SW_BLOCK + r - l, r - l)
    assert WINDOW == SW_BLOCK and n.min() >= 0 and n.max() < WINDOW
    max_exact = N_BUCKETS // 2
    large = max_exact + (np.log(np.maximum(n, 1) / max_exact)
                         / math.log(MAX_DISTANCE / max_exact)
                         * (N_BUCKETS - max_exact)).astype(np.int32)
    large = np.minimum(large, N_BUCKETS - 1)
    return np.where(n < max_exact, n, large).astype(np.int32)


_SW_GROUP = N_HEADS // (SW_KV_WIDTH // HEAD_DIM)
_SW_STACKS = tuple(
    tuple(h for par in range(2) for h in range(N_HEADS)
          if h % 2 == par and (0 if (h // _SW_GROUP) == par else 1) == variant)
    for variant in range(2))
_SW_STACK_ROWS = len(_SW_STACKS[0]) * SW_BLOCK


def _swa_bias_init(relb_ref, bucket_ref, bias_ref):
    @pl.when((pl.program_id(0) == 0) & (pl.program_id(1) == 0))
    def _():
        bucket = bucket_ref[...]
        for h in range(N_HEADS):
            t = jnp.zeros(bucket.shape, jnp.float32)
            for bkt in range(N_BUCKETS):
                t = jnp.where(bucket == bkt, relb_ref[bkt, h], t)
            bias_ref[h] = t


def _swa_tile(sink_ref, q_ref, kvp_ref, kvc_ref, bias_ref):
    i = pl.program_id(1)
    head_lanes = _head_lanes()
    lane = lax.broadcasted_iota(jnp.int32, (SW_BLOCK, SW_BLOCK), 1)
    from_prev = lane > lax.broadcasted_iota(jnp.int32, (SW_BLOCK, SW_BLOCK), 0)
    lane1 = lax.broadcasted_iota(jnp.int32, (1, SW_BLOCK), 1)
    lead_pen = jnp.where((i == 0) & (lane1 < FIRST_VALID - (LEAD - SW_BLOCK)), -jnp.inf, 0.0)
    ones_cols = [jnp.broadcast_to(jnp.where(m, 1.0, 0.0).astype(jnp.bfloat16), (2 * SW_BLOCK, LANES))
                 for m in head_lanes]
    n_half = len(_SW_STACKS[0]) // 2
    out = [[] for _ in range(N_PAIRS)]

    for sb in range(TILE // SW_BLOCK):
        rows = slice(sb * SW_BLOCK, (sb + 1) * SW_BLOCK)

        def kv_rows(blk):
            cur = kvc_ref[0, blk, :(sb + 1) * SW_BLOCK, :]
            if sb == 0:
                return jnp.concatenate([kvp_ref[0, blk], cur], axis=0)
            return cur[(sb - 1) * SW_BLOCK:]

        num = [None] * N_PAIRS
        den = [None] * N_PAIRS
        sink_term = [None] * N_HEADS
        for variant, heads in enumerate(_SW_STACKS):
            k2 = kv_rows(variant)
            v2 = kv_rows(2 + variant)
            qs = []
            for h in heads:
                q2 = q_ref[0, h // 2, rows, :]
                qs.append(jnp.where(head_lanes[h % 2], q2, jnp.zeros_like(q2)))
            logits = lax.dot_general(jnp.concatenate(qs, axis=0), k2, _NT,
                                     preferred_element_type=jnp.float32)
            weights = []
            for slot, h in enumerate(heads):
                lg = logits[slot * SW_BLOCK:(slot + 1) * SW_BLOCK]
                prev = lg[:, :SW_BLOCK]
                if sb == 0:
                    prev = prev + lead_pen
                folded = jnp.where(from_prev, prev, lg[:, SW_BLOCK:]) + bias_ref[h]
                m = jnp.maximum(jnp.max(folded, axis=-1, keepdims=True), sink_ref[h])
                m = jnp.broadcast_to(m, folded.shape)
                e = jnp.exp(folded - m)
                sink_term[h] = jnp.exp(sink_ref[h] - m)
                zero = jnp.zeros_like(e)
                weights.append(jnp.concatenate(
                    [jnp.where(from_prev, e, zero), jnp.where(from_prev, zero, e)],
                    axis=1).astype(jnp.bfloat16))
            for par in range(2):
                vm = jnp.where(head_lanes[par], v2, jnp.zeros_like(v2))
                res = jnp.dot(jnp.concatenate(weights[par * n_half:(par + 1) * n_half], axis=0),
                              jnp.concatenate([vm, ones_cols[par]], axis=1),
                              preferred_element_type=jnp.float32)
                for s in range(n_half):
                    p = heads[par * n_half + s] // 2
                    part = res[s * SW_BLOCK:(s + 1) * SW_BLOCK]
                    num[p] = part[:, :LANES] if num[p] is None else num[p] + part[:, :LANES]
                    den[p] = part[:, LANES:] if den[p] is None else den[p] + part[:, LANES:]
        for p in range(N_PAIRS):
            denom = den[p] + jnp.where(head_lanes[0], sink_term[2 * p], sink_term[2 * p + 1])
            out[p].append(num[p] / denom)
    return [jnp.concatenate(parts, axis=0) for parts in out]


def _gated_branch(g, xn, o_b, wg_ref, wb_ref):
    z = jnp.dot(xn, wg_ref[:, g * BRANCH_WIDTH:(g + 1) * BRANCH_WIDTH],
                preferred_element_type=jnp.float32)
    u = (o_b * (z * jax.nn.sigmoid(z))).astype(jnp.bfloat16)
    y = jnp.dot(u, wb_ref[g], preferred_element_type=jnp.float32)
    g0 = 2 * BRANCH_WIDTH + g * D_MODEL
    gate = jnp.dot(xn, wg_ref[:, g0:g0 + D_MODEL], preferred_element_type=jnp.float32)
    return jax.nn.sigmoid(gate) * y


def _tail_kernel(relb_ref, sink_ref, bucket_ref, q_ref, kvp_ref, kvc_ref, x_ref, oa_ref, nw_ref,
                 wg_ref, wb_ref, wo_ref, o_ref, bias_ref):
    _swa_bias_init(relb_ref, bucket_ref, bias_ref)
    x = x_ref[0]
    xn = _rms_rows(x, nw_ref[...]).astype(jnp.bfloat16)
    oa = jnp.concatenate([oa_ref[0, p] for p in range(N_PAIRS)], axis=1)
    merged = _gated_branch(0, xn, oa, wg_ref, wb_ref)
    ob = jnp.concatenate(
        _swa_tile(sink_ref, q_ref, kvp_ref, kvc_ref, bias_ref), axis=1)
    merged = merged + _gated_branch(1, xn, ob, wg_ref, wb_ref)
    o_ref[0] = x + jnp.dot(merged.astype(jnp.bfloat16), wo_ref[...],
                           preferred_element_type=jnp.float32)


def _tail(qkv, x, oa, rel_bias, sinks, norm_w, w_gate, w_branch, w_out):
    b, seq, d = x.shape
    assert all(len(s) == N_HEADS // 2 for s in _SW_STACKS)
    bucket = jnp.asarray(_t5_bucket_table())
    smem = pl.BlockSpec(memory_space=pltpu.SMEM)
    sub = TILE // SW_BLOCK
    lead_tiles = LEAD // TILE
    const2 = lambda bi, i: (0, 0)
    return pl.pallas_call(
        _tail_kernel,
        grid=(b, seq // TILE),
        in_specs=[
            smem, smem,
            pl.BlockSpec((SW_BLOCK, SW_BLOCK), const2),
            pl.BlockSpec((1, N_PAIRS, TILE, LANES), lambda bi, i: (bi, BLK_QB // N_PAIRS, i + lead_tiles, 0)),
            pl.BlockSpec((1, 4, SW_BLOCK, LANES),
                         lambda bi, i: (bi, BLK_KVB // 4, (i + lead_tiles) * sub - 1, 0)),
            pl.BlockSpec((1, 4, TILE, LANES), lambda bi, i: (bi, BLK_KVB // 4, i + lead_tiles, 0)),
            pl.BlockSpec((1, TILE, d), lambda bi, i: (bi, i, 0)),
            pl.BlockSpec((1, N_PAIRS, TILE, LANES), lambda bi, i: (bi, 0, i, 0)),
            pl.BlockSpec((1, d), const2),
            pl.BlockSpec((d, GATE_WIDTH), const2),
            pl.BlockSpec((2, BRANCH_WIDTH, d), lambda bi, i: (0, 0, 0)),
            pl.BlockSpec((d, d), const2),
        ],
        out_specs=pl.BlockSpec((1, TILE, d), lambda bi, i: (bi, i, 0)),
        out_shape=jax.ShapeDtypeStruct((b, seq, d), jnp.float32),
        scratch_shapes=[pltpu.VMEM((N_HEADS, SW_BLOCK, SW_BLOCK), jnp.float32)],
        compiler_params=pltpu.CompilerParams(
            dimension_semantics=("arbitrary", "arbitrary"), vmem_limit_bytes=VMEM_LIMIT),
        name="swa_epilogue",
    )(rel_bias, sinks, bucket, qkv, qkv, qkv, x, oa, norm_w, w_gate, w_branch, w_out)


def kernel(x, meta, rel_bias, norm_w, w_in, q_gain, k_gain, sinks, w_branch, w_out):
    b, seq, d = x.shape
    assert d == D_MODEL and norm_w.shape[0] == 1, "single-layer block of width 1024 only"
    assert meta.shape == (N_META, d) and seq % ROW_TILE == 0
    bw, kvw = BRANCH_WIDTH, SW_KV_WIDTH
    w = w_in[0]
    o_qb = 4 * bw
    o_zb, o_g = o_qb + bw + 2 * kvw, o_qb + 2 * bw + 2 * kvw
    w_qkv = jnp.concatenate([w[:, :3 * bw], w[:, o_qb:o_zb]], axis=1).astype(jnp.bfloat16)
    w_gate = jnp.concatenate([w[:, 3 * bw:4 * bw], w[:, o_zb:]], axis=1).astype(jnp.bfloat16)
    gq = jnp.tile(q_gain[0], 4)[None, :] * (HEAD_DIM ** -0.5)
    gk = jnp.tile(k_gain[0], 4)[None, :]
    nw = norm_w[0][None, :]

    qkv = _qkv_project(x, meta.astype(x.dtype), nw, w_qkv, gq, gk)
    oa = _stick_breaking(qkv, b, seq)
    return _tail(qkv, x, oa, rel_bias, sinks[0], nw, w_gate, w_branch[0].astype(jnp.bfloat16),
                 w_out[0].astype(jnp.bfloat16))
```

```python
import math

import jax
import jax.numpy as jnp
import numpy as np
from jax import lax
from jax.experimental import pallas as pl
from jax.experimental.pallas import tpu as pltpu

D_MODEL = 1024
HEAD_DIM = 64
BRANCH_WIDTH = 512
SW_KV_WIDTH = 128
N_HEADS = BRANCH_WIDTH // HEAD_DIM
N_META = 16
WINDOW = 128
SW_BLOCK = 128
N_BUCKETS = 32
MAX_DISTANCE = 128
RMS_EPS = 1e-6

LANES = 128
TILE = 256
LEAD = 2 * TILE
FIRST_VALID = LEAD - N_META
ROW_TILE = 512
N_PAIRS = BRANCH_WIDTH // LANES
QKV_PROJ_WIDTH = 4 * BRANCH_WIDTH + 2 * SW_KV_WIDTH
QKV_BLOCKS = 4 * N_PAIRS + 4
BLK_QA, BLK_KA, BLK_VA, BLK_QB, BLK_KVB = 0, N_PAIRS, 2 * N_PAIRS, 3 * N_PAIRS, 4 * N_PAIRS
GATE_WIDTH = 2 * BRANCH_WIDTH + 2 * D_MODEL
EXP_ZERO_BELOW = -105.0
VMEM_LIMIT = 56 * 1024 * 1024

_NT = (((1,), (1,)), ((), ()))


def _rms_rows(x, w):
    ms = jnp.mean(x * x, axis=-1, keepdims=True)
    return x * lax.rsqrt(ms + RMS_EPS) * w


def _exp_neg_abs(x):
    return jnp.exp2(jnp.abs(x) * (-math.log2(math.e)))


def _head_lanes():
    lane = lax.broadcasted_iota(jnp.int32, (1, LANES), 1)
    return lane < HEAD_DIM, lane >= HEAD_DIM


def _qkv_kernel(x_ref, meta_ref, nw_ref, w_ref, gq_ref, gk_ref, o_ref):
    i = pl.program_id(1)
    chunk = 2 * LANES
    r = lax.broadcasted_iota(jnp.int32, (chunk, chunk), 0) // HEAD_DIM
    c = lax.broadcasted_iota(jnp.int32, (chunk, chunk), 1) // HEAD_DIM
    head_ones = jnp.where(r == c, 1.0, 0.0).astype(jnp.bfloat16)

    def head_norm(a, gain):
        ssq = jnp.dot((a * a).astype(jnp.bfloat16), head_ones, preferred_element_type=jnp.float32)
        return a * lax.rsqrt(ssq * (1.0 / HEAD_DIM) + RMS_EPS) * gain

    def project(rows, r0, nrows):
        def emit(blk, a):
            o_ref[0, blk, r0:r0 + nrows, :] = a.astype(o_ref.dtype)

        xn = _rms_rows(rows, nw_ref[...]).astype(jnp.bfloat16)
        n_chunks = QKV_PROJ_WIDTH // chunk
        first_norm = BLK_QB // 2
        proj = lambda j: jnp.dot(xn, w_ref[:, j * chunk:(j + 1) * chunk],
                                 preferred_element_type=jnp.float32)
        to_norm = {j: proj(j) for j in range(first_norm, n_chunks)}
        for j in range(first_norm):
            a = proj(j)
            if 2 * j < BLK_KA:
                a = a * (HEAD_DIM ** -0.5)
            emit(2 * j, a[:, :LANES])
            emit(2 * j + 1, a[:, LANES:])
        for j, a in to_norm.items():
            if 2 * j < BLK_KVB:
                a = head_norm(a, gq_ref[...])
                emit(2 * j, a[:, :LANES])
                emit(2 * j + 1, a[:, LANES:])
            else:
                kb = head_norm(a, gk_ref[...])[:, :LANES]
                vb = a[:, LANES:]
                emit(BLK_KVB, kb)
                emit(BLK_KVB + 1, pltpu.roll(kb, HEAD_DIM, 1))
                emit(BLK_KVB + 2, vb)
                emit(BLK_KVB + 3, pltpu.roll(vb, HEAD_DIM, 1))

    @pl.when(i == 0)
    def _():
        o_ref[0, :, :FIRST_VALID, :] = jnp.zeros((QKV_BLOCKS, FIRST_VALID, LANES), o_ref.dtype)
        project(meta_ref[...], FIRST_VALID, N_META)

    @pl.when(i > 0)
    def _():
        project(x_ref[0], 0, ROW_TILE)


def _qkv_project(x, meta, norm_w, w_qkv, gq, gk):
    b, seq, d = x.shape
    assert LEAD == ROW_TILE and seq % ROW_TILE == 0
    n_tiles = (seq + LEAD) // ROW_TILE
    const = lambda bi, i: (0, 0)
    return pl.pallas_call(
        _qkv_kernel,
        grid=(b, n_tiles),
        in_specs=[
            pl.BlockSpec((1, ROW_TILE, d), lambda bi, i: (bi, jnp.maximum(i - 1, 0), 0)),
            pl.BlockSpec((N_META, d), const),
            pl.BlockSpec((1, d), const),
            pl.BlockSpec((d, QKV_PROJ_WIDTH), const),
            pl.BlockSpec((1, 2 * LANES), const),
            pl.BlockSpec((1, 2 * LANES), const),
        ],
        out_specs=pl.BlockSpec((1, QKV_BLOCKS, ROW_TILE, LANES), lambda bi, i: (bi, 0, i, 0)),
        out_shape=jax.ShapeDtypeStruct((b, QKV_BLOCKS, seq + LEAD, LANES), jnp.bfloat16),
        compiler_params=pltpu.CompilerParams(
            dimension_semantics=("arbitrary", "arbitrary"), vmem_limit_bytes=VMEM_LIMIT),
        name="qkv_project",
    )(x, meta, norm_w, w_qkv, gq, gk)


def _sb_kernel(q_ref, k_ref, v_ref, o_ref, carry_ref, negl_ref, logsig_ref, first_ref, w_ref):
    qt = pl.program_id(1) + LEAD // TILE
    head_lanes = _head_lanes()
    row = lax.broadcasted_iota(jnp.int32, (TILE, TILE), 0)
    col = lax.broadcasted_iota(jnp.int32, (TILE, TILE), 1)
    later_sum = jnp.where(row > col, 1.0, 0.0).astype(jnp.bfloat16)
    masks = {"diag": row > col, "lead": col >= FIRST_VALID - TILE, "full": None}

    def all_pairs(j, mode):
        mask = masks[mode]
        start = pl.multiple_of(j * TILE, TILE)
        for p in range(N_PAIRS):
            q2 = q_ref[0, p]
            k2 = k_ref[0, p, pl.ds(start, TILE), :]
            for h in range(2):
                qm = jnp.where(head_lanes[h], q2, jnp.zeros_like(q2))
                z = lax.dot_general(qm, k2, _NT, preferred_element_type=jnp.float32)
                neg_l = jnp.maximum(z, 0.0) + jnp.log(1.0 + _exp_neg_abs(z))
                if mask is not None:
                    neg_l = jnp.where(mask, neg_l, 0.0)
                negl_ref[p, h] = neg_l.astype(jnp.bfloat16)
                logsig_ref[p, h] = z - neg_l
                first_ref[p, h] = neg_l[:, :LANES]
        for p in range(N_PAIRS):
            for h in range(2):
                later = jnp.dot(negl_ref[p, h], later_sum, preferred_element_type=jnp.float32)
                total = jnp.broadcast_to(later[:, :1] + first_ref[p, h][:, :1], (TILE, LANES))
                logw = logsig_ref[p, h] - later
                if mode != "diag":
                    carry = carry_ref[p, h]
                    logw = logw - jnp.concatenate([carry] * (TILE // LANES), axis=1)
                    carry_ref[p, h] = carry + total
                else:
                    carry_ref[p, h] = total
                w = jnp.exp(logw)
                if mask is not None:
                    w = jnp.where(mask, w, 0.0)
                w_ref[p, h] = w.astype(jnp.bfloat16)
        for p in range(N_PAIRS):
            v2 = v_ref[0, p, pl.ds(start, TILE), :]
            upd = None
            for h in range(2):
                vm = jnp.where(head_lanes[h], v2, jnp.zeros_like(v2))
                pv = jnp.dot(w_ref[p, h], vm, preferred_element_type=jnp.float32)
                upd = pv if upd is None else upd + pv
            if mode == "diag":
                o_ref[0, p] = upd
            else:
                o_ref[0, p] += upd

    def still_visible():
        return jnp.min(carry_ref[...]) < -EXP_ZERO_BELOW

    all_pairs(qt, "diag")

    def cond(state):
        j, go = state
        return (j >= LEAD // TILE) & go

    def body(state):
        j, _ = state
        all_pairs(j, "full")
        return j - 1, still_visible()

    j_end, go = lax.while_loop(cond, body, (qt - 1, still_visible()))

    @pl.when(go)
    def _():
        all_pairs(LEAD // TILE - 1, "lead")


def _stick_breaking(qkv, b, seq):
    lp = seq + LEAD
    kv_spec = lambda blk: pl.BlockSpec((1, N_PAIRS, lp, LANES), lambda bi, i: (bi, blk // N_PAIRS, 0, 0),
                                       pipeline_mode=pl.Buffered(1))
    return pl.pallas_call(
        _sb_kernel,
        grid=(b, seq // TILE),
        in_specs=[
            pl.BlockSpec((1, N_PAIRS, TILE, LANES), lambda bi, i: (bi, BLK_QA // N_PAIRS, i + LEAD // TILE, 0)),
            kv_spec(BLK_KA),
            kv_spec(BLK_VA),
        ],
        out_specs=pl.BlockSpec((1, N_PAIRS, TILE, LANES), lambda bi, i: (bi, 0, i, 0)),
        out_shape=jax.ShapeDtypeStruct((b, N_PAIRS, seq, LANES), jnp.float32),
        scratch_shapes=[pltpu.VMEM((N_PAIRS, 2, TILE, LANES), jnp.float32),
                        pltpu.VMEM((N_PAIRS, 2, TILE, TILE), jnp.bfloat16),
                        pltpu.VMEM((N_PAIRS, 2, TILE, TILE), jnp.float32),
                        pltpu.VMEM((N_PAIRS, 2, TILE, LANES), jnp.float32),
                        pltpu.VMEM((N_PAIRS, 2, TILE, TILE), jnp.bfloat16)],
        compiler_params=pltpu.CompilerParams(
            dimension_semantics=("arbitrary", "arbitrary"), vmem_limit_bytes=VMEM_LIMIT),
        name="stick_breaking",
    )(qkv, qkv, qkv)


def _t5_bucket_table():
    r = np.arange(SW_BLOCK)[:, None]
    l = np.arange(SW_BLOCK)[None, :]
    n = np.where(l > r, SW_BLOCK + r - l, r - l)
    assert WINDOW == SW_BLOCK and n.min() >= 0 and n.max() < WINDOW
    max_exact = N_BUCKETS // 2
    large = max_exact + (np.log(np.maximum(n, 1) / max_exact)
                         / math.log(MAX_DISTANCE / max_exact)
                         * (N_BUCKETS - max_exact)).astype(np.int32)
    large = np.minimum(large, N_BUCKETS - 1)
    return np.where(n < max_exact, n, large).astype(np.int32)


_SW_GROUP = N_HEADS // (SW_KV_WIDTH // HEAD_DIM)
_SW_STACKS = tuple(
    tuple(h for par in range(2) for h in range(N_HEADS)
          if h % 2 == par and (0 if (h // _SW_GROUP) == par else 1) == variant)
    for variant in range(2))
_SW_STACK_ROWS = len(_SW_STACKS[0]) * SW_BLOCK


def _swa_bias_init(relb_ref, bucket_ref, bias_ref):
    @pl.when((pl.program_id(0) == 0) & (pl.program_id(1) == 0))
    def _():
        bucket = bucket_ref[...]
        for h in range(N_HEADS):
            t = jnp.zeros(bucket.shape, jnp.float32)
            for bkt in range(N_BUCKETS):
                t = jnp.where(bucket == bkt, relb_ref[bkt, h], t)
            bias_ref[h] = t


def _swa_tile(sink_ref, q_ref, kvp_ref, kvc_ref, bias_ref):
    i = pl.program_id(1)
    head_lanes = _head_lanes()
    lane = lax.broadcasted_iota(jnp.int32, (SW_BLOCK, SW_BLOCK), 1)
    from_prev = lane > lax.broadcasted_iota(jnp.int32, (SW_BLOCK, SW_BLOCK), 0)
    lane1 = lax.broadcasted_iota(jnp.int32, (1, SW_BLOCK), 1)
    lead_pen = jnp.where((i == 0) & (lane1 < FIRST_VALID - (LEAD - SW_BLOCK)), -jnp.inf, 0.0)
    ones_cols = [jnp.broadcast_to(jnp.where(m, 1.0, 0.0).astype(jnp.bfloat16), (2 * SW_BLOCK, LANES))
                 for m in head_lanes]
    n_half = len(_SW_STACKS[0]) // 2
    out = [[] for _ in range(N_PAIRS)]

    for sb in range(TILE // SW_BLOCK):
        rows = slice(sb * SW_BLOCK, (sb + 1) * SW_BLOCK)

        def kv_rows(blk):
            cur = kvc_ref[0, blk, :(sb + 1) * SW_BLOCK, :]
            if sb == 0:
                return jnp.concatenate([kvp_ref[0, blk], cur], axis=0)
            return cur[(sb - 1) * SW_BLOCK:]

        num = [None] * N_PAIRS
        den = [None] * N_PAIRS
        sink_term = [None] * N_HEADS
        for variant, heads in enumerate(_SW_STACKS):
            k2 = kv_rows(variant)
            v2 = kv_rows(2 + variant)
            qs = []
            for h in heads:
                q2 = q_ref[0, h // 2, rows, :]
                qs.append(jnp.where(head_lanes[h % 2], q2, jnp.zeros_like(q2)))
            logits = lax.dot_general(jnp.concatenate(qs, axis=0), k2, _NT,
                                     preferred_element_type=jnp.float32)
            weights = []
            for slot, h in enumerate(heads):
                lg = logits[slot * SW_BLOCK:(slot + 1) * SW_BLOCK]
                prev = lg[:, :SW_BLOCK]
                if sb == 0:
                    prev = prev + lead_pen
                folded = jnp.where(from_prev, prev, lg[:, SW_BLOCK:]) + bias_ref[h]
                m = jnp.maximum(jnp.max(folded, axis=-1, keepdims=True), sink_ref[h])
                m = jnp.broadcast_to(m, folded.shape)
                e = jnp.exp(folded - m)
                sink_term[h] = jnp.exp(sink_ref[h] - m)
                zero = jnp.zeros_like(e)
                weights.append(jnp.concatenate(
                    [jnp.where(from_prev, e, zero), jnp.where(from_prev, zero, e)],
                    axis=1).astype(jnp.bfloat16))
            for par in range(2):
                vm = jnp.where(head_lanes[par], v2, jnp.zeros_like(v2))
                res = jnp.dot(jnp.concatenate(weights[par * n_half:(par + 1) * n_half], axis=0),
                              jnp.concatenate([vm, ones_cols[par]], axis=1),
                              preferred_element_type=jnp.float32)
                for s in range(n_half):
                    p = heads[par * n_half + s] // 2
                    part = res[s * SW_BLOCK:(s + 1) * SW_BLOCK]
                    num[p] = part[:, :LANES] if num[p] is None else num[p] + part[:, :LANES]
                    den[p] = part[:, LANES:] if den[p] is None else den[p] + part[:, LANES:]
        for p in range(N_PAIRS):
            denom = den[p] + jnp.where(head_lanes[0], sink_term[2 * p], sink_term[2 * p + 1])
            out[p].append(num[p] / denom)
    return [jnp.concatenate(parts, axis=0) for parts in out]


def _gated_branch(g, xn, o_b, wg_ref, wb_ref):
    z = jnp.dot(xn, wg_ref[:, g * BRANCH_WIDTH:(g + 1) * BRANCH_WIDTH],
                preferred_element_type=jnp.float32)
    u = (o_b * (z * jax.nn.sigmoid(z))).astype(jnp.bfloat16)
    y = jnp.dot(u, wb_ref[g], preferred_element_type=jnp.float32)
    g0 = 2 * BRANCH_WIDTH + g * D_MODEL
    gate = jnp.dot(xn, wg_ref[:, g0:g0 + D_MODEL], preferred_element_type=jnp.float32)
    return jax.nn.sigmoid(gate) * y


def _tail_kernel(relb_ref, sink_ref, bucket_ref, q_ref, kvp_ref, kvc_ref, x_ref, oa_ref, nw_ref,
                 wg_ref, wb_ref, wo_ref, o_ref, bias_ref):
    _swa_bias_init(relb_ref, bucket_ref, bias_ref)
    x = x_ref[0]
    xn = _rms_rows(x, nw_ref[...]).astype(jnp.bfloat16)
    oa = jnp.concatenate([oa_ref[0, p] for p in range(N_PAIRS)], axis=1)
    merged = _gated_branch(0, xn, oa, wg_ref, wb_ref)
    ob = jnp.concatenate(
        _swa_tile(sink_ref, q_ref, kvp_ref, kvc_ref, bias_ref), axis=1)
    merged = merged + _gated_branch(1, xn, ob, wg_ref, wb_ref)
    o_ref[0] = x + jnp.dot(merged.astype(jnp.bfloat16), wo_ref[...],
                           preferred_element_type=jnp.float32)


def _tail(qkv, x, oa, rel_bias, sinks, norm_w, w_gate, w_branch, w_out):
    b, seq, d = x.shape
    assert all(len(s) == N_HEADS // 2 for s in _SW_STACKS)
    bucket = jnp.asarray(_t5_bucket_table())
    smem = pl.BlockSpec(memory_space=pltpu.SMEM)
    sub = TILE // SW_BLOCK
    lead_tiles = LEAD // TILE
    const2 = lambda bi, i: (0, 0)
    return pl.pallas_call(
        _tail_kernel,
        grid=(b, seq // TILE),
        in_specs=[
            smem, smem,
            pl.BlockSpec((SW_BLOCK, SW_BLOCK), const2),
            pl.BlockSpec((1, N_PAIRS, TILE, LANES), lambda bi, i: (bi, BLK_QB // N_PAIRS, i + lead_tiles, 0)),
            pl.BlockSpec((1, 4, SW_BLOCK, LANES),
                         lambda bi, i: (bi, BLK_KVB // 4, (i + lead_tiles) * sub - 1, 0)),
            pl.BlockSpec((1, 4, TILE, LANES), lambda bi, i: (bi, BLK_KVB // 4, i + lead_tiles, 0)),
            pl.BlockSpec((1, TILE, d), lambda bi, i: (bi, i, 0)),
            pl.BlockSpec((1, N_PAIRS, TILE, LANES), lambda bi, i: (bi, 0, i, 0)),
            pl.BlockSpec((1, d), const2),
            pl.BlockSpec((d, GATE_WIDTH), const2),
            pl.BlockSpec((2, BRANCH_WIDTH, d), lambda bi, i: (0, 0, 0)),
            pl.BlockSpec((d, d), const2),
        ],
        out_specs=pl.BlockSpec((1, TILE, d), lambda bi, i: (bi, i, 0)),
        out_shape=jax.ShapeDtypeStruct((b, seq, d), jnp.float32),
        scratch_shapes=[pltpu.VMEM((N_HEADS, SW_BLOCK, SW_BLOCK), jnp.float32)],
        compiler_params=pltpu.CompilerParams(
            dimension_semantics=("arbitrary", "arbitrary"), vmem_limit_bytes=VMEM_LIMIT),
        name="swa_epilogue",
    )(rel_bias, sinks, bucket, qkv, qkv, qkv, x, oa, norm_w, w_gate, w_branch, w_out)


def kernel(x, meta, rel_bias, norm_w, w_in, q_gain, k_gain, sinks, w_branch, w_out):
    b, seq, d = x.shape
    assert d == D_MODEL and norm_w.shape[0] == 1, "single-layer block of width 1024 only"
    assert meta.shape == (N_META, d) and seq % ROW_TILE == 0
    bw, kvw = BRANCH_WIDTH, SW_KV_WIDTH
    w = w_in[0]
    o_qb = 4 * bw
    o_zb, o_g = o_qb + bw + 2 * kvw, o_qb + 2 * bw + 2 * kvw
    w_qkv = jnp.concatenate([w[:, :3 * bw], w[:, o_qb:o_zb]], axis=1).astype(jnp.bfloat16)
    w_gate = jnp.concatenate([w[:, 3 * bw:4 * bw], w[:, o_zb:]], axis=1).astype(jnp.bfloat16)
    gq = jnp.tile(q_gain[0], 4)[None, :] * (HEAD_DIM ** -0.5)
    gk = jnp.tile(k_gain[0], 4)[None, :]
    nw = norm_w[0][None, :]

    qkv = _qkv_project(x, meta.astype(x.dtype), nw, w_qkv, gq, gk)
    oa = _stick_breaking(qkv, b, seq)
    return _tail(qkv, x, oa, rel_bias, sinks[0], nw, w_gate, w_branch[0].astype(jnp.bfloat16),
                 w_out[0].astype(jnp.bfloat16))
```

```python
import math

import jax
import jax.numpy as jnp
import numpy as np
from jax import lax
from jax.experimental import pallas as pl
from jax.experimental.pallas import tpu as pltpu

D_MODEL = 1024
HEAD_DIM = 64
BRANCH_WIDTH = 512
SW_KV_WIDTH = 128
N_HEADS = BRANCH_WIDTH // HEAD_DIM
N_META = 16
WINDOW = 128
SW_BLOCK = 128
N_BUCKETS = 32
MAX_DISTANCE = 128
RMS_EPS = 1e-6

LANES = 128
TILE = 256
LEAD = 2 * TILE
FIRST_VALID = LEAD - N_META
ROW_TILE = 512
N_PAIRS = BRANCH_WIDTH // LANES
QKV_PROJ_WIDTH = 4 * BRANCH_WIDTH + 2 * SW_KV_WIDTH
QKV_BLOCKS = 4 * N_PAIRS + 4
BLK_QA, BLK_KA, BLK_VA, BLK_QB, BLK_KVB = 0, N_PAIRS, 2 * N_PAIRS, 3 * N_PAIRS, 4 * N_PAIRS
GATE_WIDTH = 2 * BRANCH_WIDTH + 2 * D_MODEL
EXP_ZERO_BELOW = -105.0
VMEM_LIMIT = 56 * 1024 * 1024

_NT = (((1,), (1,)), ((), ()))


def _rms_rows(x, w):
    ms = jnp.mean(x * x, axis=-1, keepdims=True)
    return x * lax.rsqrt(ms + RMS_EPS) * w


def _exp_neg_abs(x):
    return jnp.exp2(jnp.abs(x) * (-math.log2(math.e)))


def _head_lanes():
    lane = lax.broadcasted_iota(jnp.int32, (1, LANES), 1)
    return lane < HEAD_DIM, lane >= HEAD_DIM


def _qkv_kernel(x_ref, meta_ref, nw_ref, w_ref, gq_ref, gk_ref, o_ref):
    i = pl.program_id(1)
    chunk = 2 * LANES
    r = lax.broadcasted_iota(jnp.int32, (chunk, chunk), 0) // HEAD_DIM
    c = lax.broadcasted_iota(jnp.int32, (chunk, chunk), 1) // HEAD_DIM
    head_ones = jnp.where(r == c, 1.0, 0.0).astype(jnp.bfloat16)

    def head_norm(a, gain):
        ssq = jnp.dot((a * a).astype(jnp.bfloat16), head_ones, preferred_element_type=jnp.float32)
        return a * lax.rsqrt(ssq * (1.0 / HEAD_DIM) + RMS_EPS) * gain

    def project(rows, r0, nrows):
        def emit(blk, a):
            o_ref[0, blk, r0:r0 + nrows, :] = a.astype(o_ref.dtype)

        xn = _rms_rows(rows, nw_ref[...]).astype(jnp.bfloat16)
        n_chunks = QKV_PROJ_WIDTH // chunk
        first_norm = BLK_QB // 2
        proj = lambda j: jnp.dot(xn, w_ref[:, j * chunk:(j + 1) * chunk],
                                 preferred_element_type=jnp.float32)
        to_norm = {j: proj(j) for j in range(first_norm, n_chunks)}
        for j in range(first_norm):
            a = proj(j)
            if 2 * j < BLK_KA:
                a = a * (HEAD_DIM ** -0.5)
            emit(2 * j, a[:, :LANES])
            emit(2 * j + 1, a[:, LANES:])
        for j, a in to_norm.items():
            if 2 * j < BLK_KVB:
                a = head_norm(a, gq_ref[...])
                emit(2 * j, a[:, :LANES])
                emit(2 * j + 1, a[:, LANES:])
            else:
                kb = head_norm(a, gk_ref[...])[:, :LANES]
                vb = a[:, LANES:]
                emit(BLK_KVB, kb)
                emit(BLK_KVB + 1, pltpu.roll(kb, HEAD_DIM, 1))
                emit(BLK_KVB + 2, vb)
                emit(BLK_KVB + 3, pltpu.roll(vb, HEAD_DIM, 1))

    @pl.when(i == 0)
    def _():
        o_ref[0, :, :FIRST_VALID, :] = jnp.zeros((QKV_BLOCKS, FIRST_VALID, LANES), o_ref.dtype)
        project(meta_ref[...], FIRST_VALID, N_META)

    @pl.when(i > 0)
    def _():
        project(x_ref[0], 0, ROW_TILE)


def _qkv_project(x, meta, norm_w, w_qkv, gq, gk):
    b, seq, d = x.shape
    assert LEAD == ROW_TILE and seq % ROW_TILE == 0
    n_tiles = (seq + LEAD) // ROW_TILE
    const = lambda bi, i: (0, 0)
    return pl.pallas_call(
        _qkv_kernel,
        grid=(b, n_tiles),
        in_specs=[
            pl.BlockSpec((1, ROW_TILE, d), lambda bi, i: (bi, jnp.maximum(i - 1, 0), 0)),
            pl.BlockSpec((N_META, d), const),
            pl.BlockSpec((1, d), const),
            pl.BlockSpec((d, QKV_PROJ_WIDTH), const),
            pl.BlockSpec((1, 2 * LANES), const),
            pl.BlockSpec((1, 2 * LANES), const),
        ],
        out_specs=pl.BlockSpec((1, QKV_BLOCKS, ROW_TILE, LANES), lambda bi, i: (bi, 0, i, 0)),
        out_shape=jax.ShapeDtypeStruct((b, QKV_BLOCKS, seq + LEAD, LANES), jnp.bfloat16),
        compiler_params=pltpu.CompilerParams(
            dimension_semantics=("arbitrary", "arbitrary"), vmem_limit_bytes=VMEM_LIMIT),
        name="qkv_project",
    )(x, meta, norm_w, w_qkv, gq, gk)


def _sb_tile(q_ref, k_ref, v_ref, acc_ref, carry_ref, negl_ref, logsig_ref, first_ref, w_ref,
             fillers=()):
    pending = list(fillers)
    qt = pl.program_id(1) + LEAD // TILE
    head_lanes = _head_lanes()
    row = lax.broadcasted_iota(jnp.int32, (TILE, TILE), 0)
    col = lax.broadcasted_iota(jnp.int32, (TILE, TILE), 1)
    later_sum = jnp.where(row > col, 1.0, 0.0).astype(jnp.bfloat16)
    masks = {"diag": row > col, "lead": col >= FIRST_VALID - TILE, "full": None,
             "prev": col >= FIRST_VALID - (qt - 1) * TILE}

    def all_pairs(j, mode, fill=False):
        mask = masks[mode]
        start = pl.multiple_of(j * TILE, TILE)
        for p in range(N_PAIRS):
            q2 = q_ref[0, p]
            k2 = k_ref[0, p, pl.ds(start, TILE), :]
            for h in range(2):
                qm = jnp.where(head_lanes[h], q2, jnp.zeros_like(q2))
                z = lax.dot_general(qm, k2, _NT, preferred_element_type=jnp.float32)
                neg_l = jnp.maximum(z, 0.0) + jnp.log(1.0 + _exp_neg_abs(z))
                if mask is not None:
                    neg_l = jnp.where(mask, neg_l, 0.0)
                negl_ref[p, h] = neg_l.astype(jnp.bfloat16)
                logsig_ref[p, h] = z - neg_l
                first_ref[p, h] = neg_l[:, :LANES]
                if fill and pending:
                    pending.pop(0)()
        for p in range(N_PAIRS):
            for h in range(2):
                later = jnp.dot(negl_ref[p, h], later_sum, preferred_element_type=jnp.float32)
                total = jnp.broadcast_to(later[:, :1] + first_ref[p, h][:, :1], (TILE, LANES))
                logw = logsig_ref[p, h] - later
                if mode != "diag":
                    carry = carry_ref[p, h]
                    logw = logw - jnp.concatenate([carry] * (TILE // LANES), axis=1)
                    carry_ref[p, h] = carry + total
                else:
                    carry_ref[p, h] = total
                w = jnp.exp(logw)
                if mask is not None:
                    w = jnp.where(mask, w, 0.0)
                w_ref[p, h] = w.astype(jnp.bfloat16)
        for p in range(N_PAIRS):
            v2 = v_ref[0, p, pl.ds(start, TILE), :]
            upd = None
            for h in range(2):
                vm = jnp.where(head_lanes[h], v2, jnp.zeros_like(v2))
                pv = jnp.dot(w_ref[p, h], vm, preferred_element_type=jnp.float32)
                upd = pv if upd is None else upd + pv
            if mode == "diag":
                acc_ref[p] = upd
            else:
                acc_ref[p] += upd

    def still_visible():
        return jnp.min(carry_ref[...]) < -EXP_ZERO_BELOW

    first_full = LEAD // TILE
    all_pairs(qt, "diag", fill=True)
    all_pairs(qt - 1, "prev", fill=True)
    while pending:
        pending.pop(0)()

    def cond(state):
        j, go = state
        return (j >= first_full) & go

    def body(state):
        j, _ = state
        all_pairs(j, "full")
        return j - 1, still_visible()

    j_end, go = lax.while_loop(cond, body, (qt - 2, still_visible()))

    @pl.when(go & (qt > first_full))
    def _():
        all_pairs(first_full - 1, "lead")


_SB_SCRATCH = [pltpu.VMEM((N_PAIRS, TILE, LANES), jnp.float32),
               pltpu.VMEM((N_PAIRS, 2, TILE, LANES), jnp.float32),
               pltpu.VMEM((N_PAIRS, 2, TILE, TILE), jnp.bfloat16),
               pltpu.VMEM((N_PAIRS, 2, TILE, TILE), jnp.float32),
               pltpu.VMEM((N_PAIRS, 2, TILE, LANES), jnp.float32),
               pltpu.VMEM((N_PAIRS, 2, TILE, TILE), jnp.bfloat16)]


def _t5_bucket_table():
    r = np.arange(SW_BLOCK)[:, None]
    l = np.arange(SW_BLOCK)[None, :]
    n = np.where(l > r, SW_BLOCK + r - l, r - l)
    assert WINDOW == SW_BLOCK and n.min() >= 0 and n.max() < WINDOW
    max_exact = N_BUCKETS // 2
    large = max_exact + (np.log(np.maximum(n, 1) / max_exact)
                         / math.log(MAX_DISTANCE / max_exact)
                         * (N_BUCKETS - max_exact)).astype(np.int32)
    large = np.minimum(large, N_BUCKETS - 1)
    return np.where(n < max_exact, n, large).astype(np.int32)


_SW_GROUP = N_HEADS // (SW_KV_WIDTH // HEAD_DIM)
_SW_STACKS = tuple(
    tuple(h for par in range(2) for h in range(N_HEADS)
          if h % 2 == par and (0 if (h // _SW_GROUP) == par else 1) == variant)
    for variant in range(2))
_SW_STACK_ROWS = len(_SW_STACKS[0]) * SW_BLOCK


def _swa_bias_init(relb_ref, bucket_ref, bias_ref):
    @pl.when((pl.program_id(0) == 0) & (pl.program_id(1) == 0))
    def _():
        bucket = bucket_ref[...]
        for h in range(N_HEADS):
            t = jnp.zeros(bucket.shape, jnp.float32)
            for bkt in range(N_BUCKETS):
                t = jnp.where(bucket == bkt, relb_ref[bkt, h], t)
            bias_ref[h] = t


def _swa_tile(sink_ref, q_ref, kvp_ref, kvc_ref, bias_ref):
    i = pl.program_id(1)
    head_lanes = _head_lanes()
    lane = lax.broadcasted_iota(jnp.int32, (SW_BLOCK, SW_BLOCK), 1)
    from_prev = lane > lax.broadcasted_iota(jnp.int32, (SW_BLOCK, SW_BLOCK), 0)
    lane1 = lax.broadcasted_iota(jnp.int32, (1, SW_BLOCK), 1)
    lead_pen = jnp.where((i == 0) & (lane1 < FIRST_VALID - (LEAD - SW_BLOCK)), -jnp.inf, 0.0)
    ones_cols = [jnp.broadcast_to(jnp.where(m, 1.0, 0.0).astype(jnp.bfloat16), (2 * SW_BLOCK, LANES))
                 for m in head_lanes]
    n_half = len(_SW_STACKS[0]) // 2
    out = [[] for _ in range(N_PAIRS)]

    for sb in range(TILE // SW_BLOCK):
        rows = slice(sb * SW_BLOCK, (sb + 1) * SW_BLOCK)

        def kv_rows(blk):
            cur = kvc_ref[0, blk, :(sb + 1) * SW_BLOCK, :]
            if sb == 0:
                return jnp.concatenate([kvp_ref[0, blk], cur], axis=0)
            return cur[(sb - 1) * SW_BLOCK:]

        num = [None] * N_PAIRS
        den = [None] * N_PAIRS
        sink_term = [None] * N_HEADS
        for variant, heads in enumerate(_SW_STACKS):
            k2 = kv_rows(variant)
            v2 = kv_rows(2 + variant)
            qs = []
            for h in heads:
                q2 = q_ref[0, h // 2, rows, :]
                qs.append(jnp.where(head_lanes[h % 2], q2, jnp.zeros_like(q2)))
            logits = lax.dot_general(jnp.concatenate(qs, axis=0), k2, _NT,
                                     preferred_element_type=jnp.float32)
            weights = []
            for slot, h in enumerate(heads):
                lg = logits[slot * SW_BLOCK:(slot + 1) * SW_BLOCK]
                prev = lg[:, :SW_BLOCK]
                if sb == 0:
                    prev = prev + lead_pen
                folded = jnp.where(from_prev, prev, lg[:, SW_BLOCK:]) + bias_ref[h]
                m = jnp.maximum(jnp.max(folded, axis=-1, keepdims=True), sink_ref[h])
                m = jnp.broadcast_to(m, folded.shape)
                e = jnp.exp(folded - m)
                sink_term[h] = jnp.exp(sink_ref[h] - m)
                zero = jnp.zeros_like(e)
                weights.append(jnp.concatenate(
                    [jnp.where(from_prev, e, zero), jnp.where(from_prev, zero, e)],
                    axis=1).astype(jnp.bfloat16))
            for par in range(2):
                vm = jnp.where(head_lanes[par], v2, jnp.zeros_like(v2))
                res = jnp.dot(jnp.concatenate(weights[par * n_half:(par + 1) * n_half], axis=0),
                              jnp.concatenate([vm, ones_cols[par]], axis=1),
                              preferred_element_type=jnp.float32)
                for s in range(n_half):
                    p = heads[par * n_half + s] // 2
                    part = res[s * SW_BLOCK:(s + 1) * SW_BLOCK]
                    num[p] = part[:, :LANES] if num[p] is None else num[p] + part[:, :LANES]
                    den[p] = part[:, LANES:] if den[p] is None else den[p] + part[:, LANES:]
        for p in range(N_PAIRS):
            denom = den[p] + jnp.where(head_lanes[0], sink_term[2 * p], sink_term[2 * p + 1])
            out[p].append(num[p] / denom)
    return [jnp.concatenate(parts, axis=0) for parts in out]


def _gated_branch(g, xn, o_b, wg_ref, wb_ref):
    z = jnp.dot(xn, wg_ref[:, g * BRANCH_WIDTH:(g + 1) * BRANCH_WIDTH],
                preferred_element_type=jnp.float32)
    u = (o_b * (z * jax.nn.sigmoid(z))).astype(jnp.bfloat16)
    y = jnp.dot(u, wb_ref[g], preferred_element_type=jnp.float32)
    g0 = 2 * BRANCH_WIDTH + g * D_MODEL
    gate = jnp.dot(xn, wg_ref[:, g0:g0 + D_MODEL], preferred_element_type=jnp.float32)
    return jax.nn.sigmoid(gate) * y


def _mix_kernel(relb_ref, sink_ref, bucket_ref, qa_ref, ka_ref, va_ref, qb_ref, kvp_ref, kvc_ref,
                x_ref, nw_ref, wg_ref, wb_ref, wo_ref, o_ref,
                bias_ref, xn_ref, act_ref, oa_ref, *sb_scratch):
    _swa_bias_init(relb_ref, bucket_ref, bias_ref)
    xn_ref[...] = _rms_rows(x_ref[0], nw_ref[...]).astype(jnp.bfloat16)
    chunk = 2 * LANES

    def gate_chunk(c):
        def run():
            cols = slice(c * chunk, (c + 1) * chunk)
            a = jnp.dot(xn_ref[...], wg_ref[:, cols], preferred_element_type=jnp.float32)
            s = jax.nn.sigmoid(a)
            act_ref[:, cols] = a * s if c * chunk < 2 * BRANCH_WIDTH else s
        return run

    _sb_tile(qa_ref, ka_ref, va_ref, oa_ref, *sb_scratch,
             fillers=[gate_chunk(c) for c in range(GATE_WIDTH // chunk)])

    bw = BRANCH_WIDTH
    ob = jnp.concatenate(_swa_tile(sink_ref, qb_ref, kvp_ref, kvc_ref, bias_ref), axis=1)
    oa = jnp.concatenate([oa_ref[p] for p in range(N_PAIRS)], axis=1)
    merged = None
    for g, o_branch in enumerate((oa, ob)):
        u = (o_branch * act_ref[:, g * bw:(g + 1) * bw]).astype(jnp.bfloat16)
        y = jnp.dot(u, wb_ref[g], preferred_element_type=jnp.float32)
        term = act_ref[:, 2 * bw + g * D_MODEL:2 * bw + (g + 1) * D_MODEL] * y
        merged = term if merged is None else merged + term
    o_ref[0] = x_ref[0] + jnp.dot(merged.astype(jnp.bfloat16), wo_ref[...],
                                  preferred_element_type=jnp.float32)


def _mix(qkv, x, rel_bias, sinks, norm_w, w_gate, w_branch, w_out):
    b, seq, d = x.shape
    assert all(len(s) == N_HEADS // 2 for s in _SW_STACKS)
    bucket = jnp.asarray(_t5_bucket_table())
    smem = pl.BlockSpec(memory_space=pltpu.SMEM)
    sub = TILE // SW_BLOCK
    lead_tiles = LEAD // TILE
    lp = seq + LEAD
    once = dict(pipeline_mode=pl.Buffered(1))
    const2 = lambda bi, i: (0, 0)
    tile_blocks = lambda blk: pl.BlockSpec((1, N_PAIRS, TILE, LANES),
                                           lambda bi, i: (bi, blk // N_PAIRS, i + lead_tiles, 0))
    all_rows = lambda blk: pl.BlockSpec((1, N_PAIRS, lp, LANES),
                                        lambda bi, i: (bi, blk // N_PAIRS, 0, 0), **once)
    return pl.pallas_call(
        _mix_kernel,
        grid=(b, seq // TILE),
        in_specs=[
            smem, smem,
            pl.BlockSpec((SW_BLOCK, SW_BLOCK), const2, **once),
            tile_blocks(BLK_QA), all_rows(BLK_KA), all_rows(BLK_VA),
            tile_blocks(BLK_QB),
            pl.BlockSpec((1, 4, SW_BLOCK, LANES),
                         lambda bi, i: (bi, BLK_KVB // 4, (i + lead_tiles) * sub - 1, 0)),
            pl.BlockSpec((1, 4, TILE, LANES), lambda bi, i: (bi, BLK_KVB // 4, i + lead_tiles, 0)),
            pl.BlockSpec((1, TILE, d), lambda bi, i: (bi, i, 0)),
            pl.BlockSpec((1, d), const2, **once),
            pl.BlockSpec((d, GATE_WIDTH), const2, **once),
            pl.BlockSpec((2, BRANCH_WIDTH, d), lambda bi, i: (0, 0, 0), **once),
            pl.BlockSpec((d, d), const2, **once),
        ],
        out_specs=pl.BlockSpec((1, TILE, d), lambda bi, i: (bi, i, 0)),
        out_shape=jax.ShapeDtypeStruct((b, seq, d), jnp.float32),
        scratch_shapes=[pltpu.VMEM((N_HEADS, SW_BLOCK, SW_BLOCK), jnp.float32),
                        pltpu.VMEM((TILE, d), jnp.bfloat16),
                        pltpu.VMEM((TILE, GATE_WIDTH), jnp.float32),
                        ] + _SB_SCRATCH,
        compiler_params=pltpu.CompilerParams(
            dimension_semantics=("arbitrary", "arbitrary"), vmem_limit_bytes=VMEM_LIMIT),
        name="attend_mix",
    )(rel_bias, sinks, bucket, qkv, qkv, qkv, qkv, qkv, qkv, x, norm_w, w_gate, w_branch, w_out)


def kernel(x, meta, rel_bias, norm_w, w_in, q_gain, k_gain, sinks, w_branch, w_out):
    b, seq, d = x.shape
    assert d == D_MODEL and norm_w.shape[0] == 1, "single-layer block of width 1024 only"
    assert meta.shape == (N_META, d) and seq % ROW_TILE == 0
    bw, kvw = BRANCH_WIDTH, SW_KV_WIDTH
    w = w_in[0]
    o_qb = 4 * bw
    o_zb, o_g = o_qb + bw + 2 * kvw, o_qb + 2 * bw + 2 * kvw
    w_qkv = jnp.concatenate([w[:, :3 * bw], w[:, o_qb:o_zb]], axis=1).astype(jnp.bfloat16)
    w_gate = jnp.concatenate([w[:, 3 * bw:4 * bw], w[:, o_zb:]], axis=1).astype(jnp.bfloat16)
    gq = jnp.tile(q_gain[0], 4)[None, :] * (HEAD_DIM ** -0.5)
    gk = jnp.tile(k_gain[0], 4)[None, :]
    nw = norm_w[0][None, :]

    qkv = _qkv_project(x, meta.astype(x.dtype), nw, w_qkv, gq, gk)
    return _mix(qkv, x, rel_bias, sinks[0], nw, w_gate, w_branch[0].astype(jnp.bfloat16),
                w_out[0].astype(jnp.bfloat16))
```

```python
import math

import jax
import jax.numpy as jnp
import numpy as np
from jax import lax
from jax.experimental import pallas as pl
from jax.experimental.pallas import tpu as pltpu

D_MODEL = 1024
HEAD_DIM = 64
BRANCH_WIDTH = 512
SW_KV_WIDTH = 128
N_HEADS = BRANCH_WIDTH // HEAD_DIM
N_META = 16
WINDOW = 128
SW_BLOCK = 128
N_BUCKETS = 32
MAX_DISTANCE = 128
RMS_EPS = 1e-6

LANES = 128
TILE = 256
LEAD = 2 * TILE
FIRST_VALID = LEAD - N_META
ROW_TILE = 512
N_PAIRS = BRANCH_WIDTH // LANES
QKV_PROJ_WIDTH = 4 * BRANCH_WIDTH + 2 * SW_KV_WIDTH
QKV_BLOCKS = 4 * N_PAIRS + 4
BLK_QA, BLK_KA, BLK_VA, BLK_QB, BLK_KVB = 0, N_PAIRS, 2 * N_PAIRS, 3 * N_PAIRS, 4 * N_PAIRS
GATE_WIDTH = 2 * BRANCH_WIDTH + 2 * D_MODEL
W_CHUNK = 2 * LANES
_QKV_SRC_CHUNKS = (0, 1, 2, 3, 4, 5, 8, 9, 10)
_QKV_PIECES = ((0, 6), (8, 4))
_GATE_SRC_CHUNKS = (6, 7) + tuple(range(11, 21))
_GATE_PIECES = ((6, 2), (11, 1), (12, 4), (16, 4), (20, 1))
EXP_ZERO_BELOW = -105.0
VMEM_LIMIT = 56 * 1024 * 1024

_NT = (((1,), (1,)), ((), ()))


def _rms_rows(x, w):
    ms = jnp.mean(x * x, axis=-1, keepdims=True)
    return x * lax.rsqrt(ms + RMS_EPS) * w


def _exp_neg_abs(x):
    return jnp.exp2(jnp.abs(x) * (-math.log2(math.e)))


def _head_lanes():
    lane = lax.broadcasted_iota(jnp.int32, (1, LANES), 1)
    return lane < HEAD_DIM, lane >= HEAD_DIM


def _w_chunk(pieces, c):
    for ref, first, count in pieces:
        if first <= c < first + count:
            off = (c - first) * W_CHUNK
            return ref[0, :, off:off + W_CHUNK].astype(jnp.bfloat16)
    raise ValueError(c)


def _w_piece_spec(d, first, count):
    assert first % count == 0
    return pl.BlockSpec((1, d, count * W_CHUNK), lambda bi, i: (0, 0, first // count),
                        pipeline_mode=pl.Buffered(1))


def _qkv_kernel(x_ref, meta_ref, nw_ref, wa_ref, wb_ref, gq_ref, gk_ref, o_ref):
    i = pl.program_id(1)
    chunk = W_CHUNK
    pieces = [(wa_ref,) + _QKV_PIECES[0], (wb_ref,) + _QKV_PIECES[1]]
    r = lax.broadcasted_iota(jnp.int32, (chunk, chunk), 0) // HEAD_DIM
    c = lax.broadcasted_iota(jnp.int32, (chunk, chunk), 1) // HEAD_DIM
    head_ones = jnp.where(r == c, 1.0, 0.0).astype(jnp.bfloat16)

    def head_norm(a, gain):
        ssq = jnp.dot((a * a).astype(jnp.bfloat16), head_ones, preferred_element_type=jnp.float32)
        return a * lax.rsqrt(ssq * (1.0 / HEAD_DIM) + RMS_EPS) * gain

    def project(rows, r0, nrows):
        def emit(blk, a):
            o_ref[0, blk, r0:r0 + nrows, :] = a.astype(o_ref.dtype)

        xn = _rms_rows(rows, nw_ref[...]).astype(jnp.bfloat16)
        n_chunks = QKV_PROJ_WIDTH // chunk
        first_norm = BLK_QB // 2
        proj = lambda j: jnp.dot(xn, _w_chunk(pieces, _QKV_SRC_CHUNKS[j]),
                                 preferred_element_type=jnp.float32)
        to_norm = {j: proj(j) for j in range(first_norm, n_chunks)}
        for j in range(first_norm):
            a = proj(j)
            if 2 * j < BLK_KA:
                a = a * (HEAD_DIM ** -0.5)
            emit(2 * j, a[:, :LANES])
            emit(2 * j + 1, a[:, LANES:])
        for j, a in to_norm.items():
            if 2 * j < BLK_KVB:
                a = head_norm(a, gq_ref[...])
                emit(2 * j, a[:, :LANES])
                emit(2 * j + 1, a[:, LANES:])
            else:
                kb = head_norm(a, gk_ref[...])[:, :LANES]
                vb = a[:, LANES:]
                emit(BLK_KVB, kb)
                emit(BLK_KVB + 1, pltpu.roll(kb, HEAD_DIM, 1))
                emit(BLK_KVB + 2, vb)
                emit(BLK_KVB + 3, pltpu.roll(vb, HEAD_DIM, 1))

    @pl.when(i == 0)
    def _():
        o_ref[0, :, :FIRST_VALID, :] = jnp.zeros((QKV_BLOCKS, FIRST_VALID, LANES), o_ref.dtype)
        project(meta_ref[...], FIRST_VALID, N_META)

    @pl.when(i > 0)
    def _():
        project(x_ref[0], 0, ROW_TILE)


def _qkv_project(x, meta, norm_w, w_in, gq, gk):
    b, seq, d = x.shape
    assert LEAD == ROW_TILE and seq % ROW_TILE == 0
    n_tiles = (seq + LEAD) // ROW_TILE
    const = lambda bi, i: (0, 0)
    return pl.pallas_call(
        _qkv_kernel,
        grid=(b, n_tiles),
        in_specs=[
            pl.BlockSpec((1, ROW_TILE, d), lambda bi, i: (bi, jnp.maximum(i - 1, 0), 0)),
            pl.BlockSpec((N_META, d), const),
            pl.BlockSpec((1, d), const),
            _w_piece_spec(d, *_QKV_PIECES[0]),
            _w_piece_spec(d, *_QKV_PIECES[1]),
            pl.BlockSpec((1, 2 * LANES), const),
            pl.BlockSpec((1, 2 * LANES), const),
        ],
        out_specs=pl.BlockSpec((1, QKV_BLOCKS, ROW_TILE, LANES), lambda bi, i: (bi, 0, i, 0)),
        out_shape=jax.ShapeDtypeStruct((b, QKV_BLOCKS, seq + LEAD, LANES), jnp.bfloat16),
        compiler_params=pltpu.CompilerParams(
            dimension_semantics=("arbitrary", "arbitrary"), vmem_limit_bytes=VMEM_LIMIT),
        name="qkv_project",
    )(x, meta, norm_w, w_in, w_in, gq, gk)


def _sb_tile(q_ref, k_ref, v_ref, acc_ref, carry_ref, negl_ref, logsig_ref, first_ref, w_ref,
             fillers=()):
    pending = list(fillers)
    qt = pl.program_id(1) + LEAD // TILE
    head_lanes = _head_lanes()
    row = lax.broadcasted_iota(jnp.int32, (TILE, TILE), 0)
    col = lax.broadcasted_iota(jnp.int32, (TILE, TILE), 1)
    later_sum = jnp.where(row > col, 1.0, 0.0).astype(jnp.bfloat16)
    masks = {"diag": row > col, "lead": col >= FIRST_VALID - TILE, "full": None,
             "prev": col >= FIRST_VALID - (qt - 1) * TILE}

    def all_pairs(j, mode, fill=False):
        mask = masks[mode]
        start = pl.multiple_of(j * TILE, TILE)
        for p in range(N_PAIRS):
            q2 = q_ref[0, p]
            k2 = k_ref[0, p, pl.ds(start, TILE), :]
            for h in range(2):
                qm = jnp.where(head_lanes[h], q2, jnp.zeros_like(q2))
                z = lax.dot_general(qm, k2, _NT, preferred_element_type=jnp.float32)
                neg_l = jnp.maximum(z, 0.0) + jnp.log(1.0 + _exp_neg_abs(z))
                if mask is not None:
                    neg_l = jnp.where(mask, neg_l, 0.0)
                negl_ref[p, h] = neg_l.astype(jnp.bfloat16)
                logsig_ref[p, h] = z - neg_l
                first_ref[p, h] = neg_l[:, :LANES]
                if fill and pending:
                    pending.pop(0)()
        for p in range(N_PAIRS):
            for h in range(2):
                later = jnp.dot(negl_ref[p, h], later_sum, preferred_element_type=jnp.float32)
                total = jnp.broadcast_to(later[:, :1] + first_ref[p, h][:, :1], (TILE, LANES))
                logw = logsig_ref[p, h] - later
                if mode != "diag":
                    carry = carry_ref[p, h]
                    logw = logw - jnp.concatenate([carry] * (TILE // LANES), axis=1)
                    carry_ref[p, h] = carry + total
                else:
                    carry_ref[p, h] = total
                w = jnp.exp(logw)
                if mask is not None:
                    w = jnp.where(mask, w, 0.0)
                w_ref[p, h] = w.astype(jnp.bfloat16)
        for p in range(N_PAIRS):
            v2 = v_ref[0, p, pl.ds(start, TILE), :]
            upd = None
            for h in range(2):
                vm = jnp.where(head_lanes[h], v2, jnp.zeros_like(v2))
                pv = jnp.dot(w_ref[p, h], vm, preferred_element_type=jnp.float32)
                upd = pv if upd is None else upd + pv
            if mode == "diag":
                acc_ref[p] = upd
            else:
                acc_ref[p] += upd

    def still_visible():
        return jnp.min(carry_ref[...]) < -EXP_ZERO_BELOW

    first_full = LEAD // TILE
    all_pairs(qt, "diag", fill=True)
    all_pairs(qt - 1, "prev", fill=True)
    while pending:
        pending.pop(0)()

    def cond(state):
        j, go = state
        return (j >= first_full) & go

    def body(state):
        j, _ = state
        all_pairs(j, "full")
        return j - 1, still_visible()

    j_end, go = lax.while_loop(cond, body, (qt - 2, still_visible()))

    @pl.when(go & (qt > first_full))
    def _():
        all_pairs(first_full - 1, "lead")


_SB_SCRATCH = [pltpu.VMEM((N_PAIRS, TILE, LANES), jnp.float32),
               pltpu.VMEM((N_PAIRS, 2, TILE, LANES), jnp.float32),
               pltpu.VMEM((N_PAIRS, 2, TILE, TILE), jnp.bfloat16),
               pltpu.VMEM((N_PAIRS, 2, TILE, TILE), jnp.float32),
               pltpu.VMEM((N_PAIRS, 2, TILE, LANES), jnp.float32),
               pltpu.VMEM((N_PAIRS, 2, TILE, TILE), jnp.bfloat16)]


def _t5_bucket_table():
    r = np.arange(SW_BLOCK)[:, None]
    l = np.arange(SW_BLOCK)[None, :]
    n = np.where(l > r, SW_BLOCK + r - l, r - l)
    assert WINDOW == SW_BLOCK and n.min() >= 0 and n.max() < WINDOW
    max_exact = N_BUCKETS // 2
    large = max_exact + (np.log(np.maximum(n, 1) / max_exact)
                         / math.log(MAX_DISTANCE / max_exact)
                         * (N_BUCKETS - max_exact)).astype(np.int32)
    large = np.minimum(large, N_BUCKETS - 1)
    return np.where(n < max_exact, n, large).astype(np.int32)


_SW_GROUP = N_HEADS // (SW_KV_WIDTH // HEAD_DIM)
_SW_STACKS = tuple(
    tuple(h for par in range(2) for h in range(N_HEADS)
          if h % 2 == par and (0 if (h // _SW_GROUP) == par else 1) == variant)
    for variant in range(2))
_SW_STACK_ROWS = len(_SW_STACKS[0]) * SW_BLOCK


def _swa_bias_init(relb_ref, bucket_ref, bias_ref):
    @pl.when((pl.program_id(0) == 0) & (pl.program_id(1) == 0))
    def _():
        bucket = bucket_ref[...]
        for h in range(N_HEADS):
            t = jnp.zeros(bucket.shape, jnp.float32)
            for bkt in range(N_BUCKETS):
                t = jnp.where(bucket == bkt, relb_ref[bkt, h], t)
            bias_ref[h] = t


def _swa_block(sb, sink_ref, q_ref, kvp_ref, kvc_ref, bias_ref, ob_ref):
    i = pl.program_id(1)
    head_lanes = _head_lanes()
    lane = lax.broadcasted_iota(jnp.int32, (SW_BLOCK, SW_BLOCK), 1)
    from_prev = lane > lax.broadcasted_iota(jnp.int32, (SW_BLOCK, SW_BLOCK), 0)
    lane1 = lax.broadcasted_iota(jnp.int32, (1, SW_BLOCK), 1)
    lead_pen = jnp.where((i == 0) & (lane1 < FIRST_VALID - (LEAD - SW_BLOCK)), -jnp.inf, 0.0)
    ones_cols = [jnp.broadcast_to(jnp.where(m, 1.0, 0.0).astype(jnp.bfloat16), (2 * SW_BLOCK, LANES))
                 for m in head_lanes]
    n_half = len(_SW_STACKS[0]) // 2
    rows = slice(sb * SW_BLOCK, (sb + 1) * SW_BLOCK)

    def kv_rows(blk):
        cur = kvc_ref[0, blk, :(sb + 1) * SW_BLOCK, :]
        if sb == 0:
            return jnp.concatenate([kvp_ref[0, blk], cur], axis=0)
        return cur[(sb - 1) * SW_BLOCK:]

    num = [None] * N_PAIRS
    den = [None] * N_PAIRS
    sink_term = [None] * N_HEADS
    for variant, heads in enumerate(_SW_STACKS):
        k2 = kv_rows(variant)
        v2 = kv_rows(2 + variant)
        qs = []
        for h in heads:
            q2 = q_ref[0, h // 2, rows, :]
            qs.append(jnp.where(head_lanes[h % 2], q2, jnp.zeros_like(q2)))
        logits = lax.dot_general(jnp.concatenate(qs, axis=0), k2, _NT,
                                 preferred_element_type=jnp.float32)
        weights = []
        for slot, h in enumerate(heads):
            lg = logits[slot * SW_BLOCK:(slot + 1) * SW_BLOCK]
            prev = lg[:, :SW_BLOCK]
            if sb == 0:
                prev = prev + lead_pen
            folded = jnp.where(from_prev, prev, lg[:, SW_BLOCK:]) + bias_ref[h]
            m = jnp.maximum(jnp.max(folded, axis=-1, keepdims=True), sink_ref[h])
            m = jnp.broadcast_to(m, folded.shape)
            e = jnp.exp(folded - m)
            sink_term[h] = jnp.exp(sink_ref[h] - m)
            zero = jnp.zeros_like(e)
            weights.append(jnp.concatenate(
                [jnp.where(from_prev, e, zero), jnp.where(from_prev, zero, e)],
                axis=1).astype(jnp.bfloat16))
        for par in range(2):
            vm = jnp.where(head_lanes[par], v2, jnp.zeros_like(v2))
            res = jnp.dot(jnp.concatenate(weights[par * n_half:(par + 1) * n_half], axis=0),
                          jnp.concatenate([vm, ones_cols[par]], axis=1),
                          preferred_element_type=jnp.float32)
            for s in range(n_half):
                p = heads[par * n_half + s] // 2
                part = res[s * SW_BLOCK:(s + 1) * SW_BLOCK]
                num[p] = part[:, :LANES] if num[p] is None else num[p] + part[:, :LANES]
                den[p] = part[:, LANES:] if den[p] is None else den[p] + part[:, LANES:]
    for p in range(N_PAIRS):
        denom = den[p] + jnp.where(head_lanes[0], sink_term[2 * p], sink_term[2 * p + 1])
        ob_ref[rows, p * LANES:(p + 1) * LANES] = num[p] / denom


def _mix_kernel(relb_ref, sink_ref, bucket_ref, qa_ref, ka_ref, va_ref, qb_ref, kvp_ref, kvc_ref,
                x_ref, nw_ref, wg0_ref, wg1_ref, wg2_ref, wg3_ref, wg4_ref, wb_ref, wo_ref, o_ref,
                bias_ref, xn_ref, act_ref, ob_ref, oa_ref, *sb_scratch):
    _swa_bias_init(relb_ref, bucket_ref, bias_ref)
    xn_ref[...] = _rms_rows(x_ref[0], nw_ref[...]).astype(jnp.bfloat16)
    chunk = W_CHUNK
    bw = BRANCH_WIDTH
    pieces = [(r,) + p for r, p in zip((wg0_ref, wg1_ref, wg2_ref, wg3_ref, wg4_ref), _GATE_PIECES)]

    def gate_chunk(c):
        def run():
            cols = slice(c * chunk, (c + 1) * chunk)
            a = jnp.dot(xn_ref[...], _w_chunk(pieces, _GATE_SRC_CHUNKS[c]),
                        preferred_element_type=jnp.float32)
            s = jax.nn.sigmoid(a)
            act_ref[:, cols] = a * s if c * chunk < 2 * bw else s
        return run

    _sb_tile(qa_ref, ka_ref, va_ref, oa_ref, *sb_scratch,
             fillers=[gate_chunk(c) for c in range(GATE_WIDTH // chunk)])
    for sb in range(TILE // SW_BLOCK):
        _swa_block(sb, sink_ref, qb_ref, kvp_ref, kvc_ref, bias_ref, ob_ref)
    oa = jnp.concatenate([oa_ref[p] for p in range(N_PAIRS)], axis=1)
    merged = None
    for g, o_branch in enumerate((oa, ob_ref[...])):
        u = (o_branch * act_ref[:, g * bw:(g + 1) * bw]).astype(jnp.bfloat16)
        y = jnp.dot(u, wb_ref[g], preferred_element_type=jnp.float32)
        term = act_ref[:, 2 * bw + g * D_MODEL:2 * bw + (g + 1) * D_MODEL] * y
        merged = term if merged is None else merged + term
    o_ref[0] = x_ref[0] + jnp.dot(merged.astype(jnp.bfloat16), wo_ref[...],
                                  preferred_element_type=jnp.float32)


def _mix(qkv, x, rel_bias, sinks, norm_w, w_in, w_branch, w_out):
    b, seq, d = x.shape
    assert all(len(s) == N_HEADS // 2 for s in _SW_STACKS)
    bucket = jnp.asarray(_t5_bucket_table())
    smem = pl.BlockSpec(memory_space=pltpu.SMEM)
    sub = TILE // SW_BLOCK
    lead_tiles = LEAD // TILE
    lp = seq + LEAD
    once = dict(pipeline_mode=pl.Buffered(1))
    const2 = lambda bi, i: (0, 0)
    tile_blocks = lambda blk: pl.BlockSpec((1, N_PAIRS, TILE, LANES),
                                           lambda bi, i: (bi, blk // N_PAIRS, i + lead_tiles, 0))
    all_rows = lambda blk: pl.BlockSpec((1, N_PAIRS, lp, LANES),
                                        lambda bi, i: (bi, blk // N_PAIRS, 0, 0), **once)
    return pl.pallas_call(
        _mix_kernel,
        grid=(b, seq // TILE),
        in_specs=[
            smem, smem,
            pl.BlockSpec((SW_BLOCK, SW_BLOCK), const2, **once),
            tile_blocks(BLK_QA), all_rows(BLK_KA), all_rows(BLK_VA),
            tile_blocks(BLK_QB),
            pl.BlockSpec((1, 4, SW_BLOCK, LANES),
                         lambda bi, i: (bi, BLK_KVB // 4, (i + lead_tiles) * sub - 1, 0)),
            pl.BlockSpec((1, 4, TILE, LANES), lambda bi, i: (bi, BLK_KVB // 4, i + lead_tiles, 0)),
            pl.BlockSpec((1, TILE, d), lambda bi, i: (bi, i, 0)),
            pl.BlockSpec((1, d), const2, **once),
            *[_w_piece_spec(d, *piece) for piece in _GATE_PIECES],
            pl.BlockSpec((2, BRANCH_WIDTH, d), lambda bi, i: (0, 0, 0), **once),
            pl.BlockSpec((d, d), const2, **once),
        ],
        out_specs=pl.BlockSpec((1, TILE, d), lambda bi, i: (bi, i, 0)),
        out_shape=jax.ShapeDtypeStruct((b, seq, d), jnp.float32),
        scratch_shapes=[pltpu.VMEM((N_HEADS, SW_BLOCK, SW_BLOCK), jnp.float32),
                        pltpu.VMEM((TILE, d), jnp.bfloat16),
                        pltpu.VMEM((TILE, GATE_WIDTH), jnp.float32),
                        pltpu.VMEM((TILE, BRANCH_WIDTH), jnp.float32),
                        ] + _SB_SCRATCH,
        compiler_params=pltpu.CompilerParams(
            dimension_semantics=("arbitrary", "arbitrary"), vmem_limit_bytes=VMEM_LIMIT),
        name="attend_mix",
    )(rel_bias, sinks, bucket, qkv, qkv, qkv, qkv, qkv, qkv, x, norm_w,
      *([w_in] * len(_GATE_PIECES)), w_branch, w_out)


def kernel(x, meta, rel_bias, norm_w, w_in, q_gain, k_gain, sinks, w_branch, w_out):
    b, seq, d = x.shape
    assert d == D_MODEL and norm_w.shape[0] == 1, "single-layer block of width 1024 only"
    assert meta.shape == (N_META, d) and seq % ROW_TILE == 0
    assert w_in.shape == (1, d, 4 * BRANCH_WIDTH + 2 * SW_KV_WIDTH + GATE_WIDTH)
    gq = jnp.tile(q_gain[0], 4)[None, :] * (HEAD_DIM ** -0.5)
    gk = jnp.tile(k_gain[0], 4)[None, :]
    nw = norm_w[0][None, :]

    qkv = _qkv_project(x, meta.astype(x.dtype), nw, w_in, gq, gk)
    return _mix(qkv, x, rel_bias, sinks[0], nw, w_in, w_branch[0].astype(jnp.bfloat16),
                w_out[0].astype(jnp.bfloat16))
```

```python
import math

import jax
import jax.numpy as jnp
import numpy as np
from jax import lax
from jax.experimental import pallas as pl
from jax.experimental.pallas import tpu as pltpu

D_MODEL = 1024
HEAD_DIM = 64
BRANCH_WIDTH = 512
SW_KV_WIDTH = 128
N_HEADS = BRANCH_WIDTH // HEAD_DIM
N_META = 16
WINDOW = 128
SW_BLOCK = 128
N_BUCKETS = 32
MAX_DISTANCE = 128
RMS_EPS = 1e-6

LANES = 128
TILE = 256
LEAD = 2 * TILE
FIRST_VALID = LEAD - N_META
ROW_TILE = 512
N_PAIRS = BRANCH_WIDTH // LANES
QKV_PROJ_WIDTH = 4 * BRANCH_WIDTH + 2 * SW_KV_WIDTH
QKV_BLOCKS = 4 * N_PAIRS + 4
BLK_QA, BLK_KA, BLK_VA, BLK_QB, BLK_KVB = 0, N_PAIRS, 2 * N_PAIRS, 3 * N_PAIRS, 4 * N_PAIRS
GATE_WIDTH = 2 * BRANCH_WIDTH + 2 * D_MODEL
W_CHUNK = 2 * LANES
_QKV_SRC_CHUNKS = (0, 1, 2, 3, 4, 5, 8, 9, 10)
_QKV_PIECES = ((0, 6), (8, 4))
_GATE_SRC_CHUNKS = (6, 7) + tuple(range(11, 21))
_GATE_PIECES = ((6, 2), (11, 1), (12, 4), (16, 4), (20, 1))
EXP_ZERO_BELOW = -105.0
VMEM_LIMIT = 56 * 1024 * 1024

_NT = (((1,), (1,)), ((), ()))


def _rms_rows(x, w):
    ms = jnp.mean(x * x, axis=-1, keepdims=True)
    return x * lax.rsqrt(ms + RMS_EPS) * w


def _exp_neg_abs(x):
    return jnp.exp2(jnp.abs(x) * (-math.log2(math.e)))


def _head_lanes():
    lane = lax.broadcasted_iota(jnp.int32, (1, LANES), 1)
    return lane < HEAD_DIM, lane >= HEAD_DIM


def _w_chunk(pieces, c):
    for ref, first, count in pieces:
        if first <= c < first + count:
            off = (c - first) * W_CHUNK
            return ref[0, :, off:off + W_CHUNK].astype(jnp.bfloat16)
    raise ValueError(c)


def _w_piece_spec(d, first, count):
    assert first % count == 0
    return pl.BlockSpec((1, d, count * W_CHUNK), lambda bi, i: (0, 0, first // count),
                        pipeline_mode=pl.Buffered(1))


def _qkv_kernel(x_ref, meta_ref, nw_ref, wa_ref, wb_ref, gq_ref, gk_ref, o_ref):
    i = pl.program_id(1)
    chunk = W_CHUNK
    pieces = [(wa_ref,) + _QKV_PIECES[0], (wb_ref,) + _QKV_PIECES[1]]
    r = lax.broadcasted_iota(jnp.int32, (chunk, chunk), 0) // HEAD_DIM
    c = lax.broadcasted_iota(jnp.int32, (chunk, chunk), 1) // HEAD_DIM
    head_ones = jnp.where(r == c, 1.0, 0.0).astype(jnp.bfloat16)

    def head_norm(a, gain):
        ssq = jnp.dot((a * a).astype(jnp.bfloat16), head_ones, preferred_element_type=jnp.float32)
        return a * lax.rsqrt(ssq * (1.0 / HEAD_DIM) + RMS_EPS) * gain

    def project(rows, r0, nrows):
        def emit(blk, a):
            o_ref[0, blk, r0:r0 + nrows, :] = a.astype(o_ref.dtype)

        xn = _rms_rows(rows, nw_ref[...]).astype(jnp.bfloat16)
        n_chunks = QKV_PROJ_WIDTH // chunk
        first_norm = BLK_QB // 2
        proj = lambda j: jnp.dot(xn, _w_chunk(pieces, _QKV_SRC_CHUNKS[j]),
                                 preferred_element_type=jnp.float32)
        to_norm = {j: proj(j) for j in range(first_norm, n_chunks)}
        for j in range(first_norm):
            a = proj(j)
            if 2 * j < BLK_KA:
                a = a * (HEAD_DIM ** -0.5)
            emit(2 * j, a[:, :LANES])
            emit(2 * j + 1, a[:, LANES:])
        for j, a in to_norm.items():
            if 2 * j < BLK_KVB:
                a = head_norm(a, gq_ref[...])
                emit(2 * j, a[:, :LANES])
                emit(2 * j + 1, a[:, LANES:])
            else:
                kb = head_norm(a, gk_ref[...])[:, :LANES]
                vb = a[:, LANES:]
                emit(BLK_KVB, kb)
                emit(BLK_KVB + 1, pltpu.roll(kb, HEAD_DIM, 1))
                emit(BLK_KVB + 2, vb)
                emit(BLK_KVB + 3, pltpu.roll(vb, HEAD_DIM, 1))

    @pl.when(i == 0)
    def _():
        o_ref[0, :, :FIRST_VALID, :] = jnp.zeros((QKV_BLOCKS, FIRST_VALID, LANES), o_ref.dtype)
        project(meta_ref[...], FIRST_VALID, N_META)

    @pl.when(i > 0)
    def _():
        project(x_ref[0], 0, ROW_TILE)


def _qkv_project(x, meta, norm_w, w_in, gq, gk):
    b, seq, d = x.shape
    assert LEAD == ROW_TILE and seq % ROW_TILE == 0
    n_tiles = (seq + LEAD) // ROW_TILE
    const = lambda bi, i: (0, 0)
    return pl.pallas_call(
        _qkv_kernel,
        grid=(b, n_tiles),
        in_specs=[
            pl.BlockSpec((1, ROW_TILE, d), lambda bi, i: (bi, jnp.maximum(i - 1, 0), 0)),
            pl.BlockSpec((N_META, d), const),
            pl.BlockSpec((1, d), const),
            _w_piece_spec(d, *_QKV_PIECES[0]),
            _w_piece_spec(d, *_QKV_PIECES[1]),
            pl.BlockSpec((1, 2 * LANES), const),
            pl.BlockSpec((1, 2 * LANES), const),
        ],
        out_specs=pl.BlockSpec((1, QKV_BLOCKS, ROW_TILE, LANES), lambda bi, i: (bi, 0, i, 0)),
        out_shape=jax.ShapeDtypeStruct((b, QKV_BLOCKS, seq + LEAD, LANES), jnp.bfloat16),
        compiler_params=pltpu.CompilerParams(
            dimension_semantics=("arbitrary", "arbitrary"), vmem_limit_bytes=VMEM_LIMIT),
        name="qkv_project",
    )(x, meta, norm_w, w_in, w_in, gq, gk)


def _sb_tile(q_ref, k_ref, v_ref, acc_ref, carry_ref, negl_ref, logsig_ref, first_ref, w_ref,
             fillers=()):
    pending = list(fillers)
    qt = pl.program_id(1) + LEAD // TILE
    head_lanes = _head_lanes()
    row = lax.broadcasted_iota(jnp.int32, (TILE, TILE), 0)
    col = lax.broadcasted_iota(jnp.int32, (TILE, TILE), 1)
    later_sum = jnp.where(row > col, 1.0, 0.0).astype(jnp.bfloat16)
    masks = {"diag": row > col, "lead": col >= FIRST_VALID - TILE, "full": None,
             "prev": col >= FIRST_VALID - (qt - 1) * TILE}

    def all_pairs(j, mode, fill=False):
        mask = masks[mode]
        start = pl.multiple_of(j * TILE, TILE)
        for p in range(N_PAIRS):
            q2 = q_ref[0, p]
            k2 = k_ref[0, p, pl.ds(start, TILE), :]
            for h in range(2):
                qm = jnp.where(head_lanes[h], q2, jnp.zeros_like(q2))
                z = lax.dot_general(qm, k2, _NT, preferred_element_type=jnp.float32)
                neg_l = jnp.maximum(z, 0.0) + jnp.log(1.0 + _exp_neg_abs(z))
                if mask is not None:
                    neg_l = jnp.where(mask, neg_l, 0.0)
                negl_ref[p, h] = neg_l.astype(jnp.bfloat16)
                logsig_ref[p, h] = z - neg_l
                first_ref[p, h] = neg_l[:, :LANES]
                if fill and pending and (2 * p + h) % 2 == 1:
                    pending.pop(0)()
        for p in range(N_PAIRS):
            for h in range(2):
                later = jnp.dot(negl_ref[p, h], later_sum, preferred_element_type=jnp.float32)
                total = jnp.broadcast_to(later[:, :1] + first_ref[p, h][:, :1], (TILE, LANES))
                logw = logsig_ref[p, h] - later
                if mode != "diag":
                    carry = carry_ref[p, h]
                    logw = logw - jnp.concatenate([carry] * (TILE // LANES), axis=1)
                    carry_ref[p, h] = carry + total
                else:
                    carry_ref[p, h] = total
                w = jnp.exp(logw)
                if mask is not None:
                    w = jnp.where(mask, w, 0.0)
                w_ref[p, h] = w.astype(jnp.bfloat16)
                if fill and pending and (2 * p + h) % 4 == 3:
                    pending.pop(0)()
        for p in range(N_PAIRS):
            v2 = v_ref[0, p, pl.ds(start, TILE), :]
            upd = None
            for h in range(2):
                vm = jnp.where(head_lanes[h], v2, jnp.zeros_like(v2))
                pv = jnp.dot(w_ref[p, h], vm, preferred_element_type=jnp.float32)
                upd = pv if upd is None else upd + pv
            if mode == "diag":
                acc_ref[p] = upd
            else:
                acc_ref[p] += upd

    def still_visible():
        return jnp.min(carry_ref[...]) < -EXP_ZERO_BELOW

    first_full = LEAD // TILE
    all_pairs(qt, "diag", fill=True)
    all_pairs(qt - 1, "prev", fill=True)
    while pending:
        pending.pop(0)()

    def cond(state):
        j, go = state
        return (j >= first_full) & go

    def body(state):
        j, _ = state
        all_pairs(j, "full")
        return j - 1, still_visible()

    j_end, go = lax.while_loop(cond, body, (qt - 2, still_visible()))

    @pl.when(go & (qt > first_full))
    def _():
        all_pairs(first_full - 1, "lead")


_SB_SCRATCH = [pltpu.VMEM((N_PAIRS, TILE, LANES), jnp.float32),
               pltpu.VMEM((N_PAIRS, 2, TILE, LANES), jnp.float32),
               pltpu.VMEM((N_PAIRS, 2, TILE, TILE), jnp.bfloat16),
               pltpu.VMEM((N_PAIRS, 2, TILE, TILE), jnp.float32),
               pltpu.VMEM((N_PAIRS, 2, TILE, LANES), jnp.float32),
               pltpu.VMEM((N_PAIRS, 2, TILE, TILE), jnp.bfloat16)]


def _t5_bucket_table():
    r = np.arange(SW_BLOCK)[:, None]
    l = np.arange(SW_BLOCK)[None, :]
    n = np.where(l > r, SW_BLOCK + r - l, r - l)
    assert WINDOW == SW_BLOCK and n.min() >= 0 and n.max() < WINDOW
    max_exact = N_BUCKETS // 2
    large = max_exact + (np.log(np.maximum(n, 1) / max_exact)
                         / math.log(MAX_DISTANCE / max_exact)
                         * (N_BUCKETS - max_exact)).astype(np.int32)
    large = np.minimum(large, N_BUCKETS - 1)
    return np.where(n < max_exact, n, large).astype(np.int32)


_SW_GROUP = N_HEADS // (SW_KV_WIDTH // HEAD_DIM)
_SW_STACKS = tuple(
    tuple(h for par in range(2) for h in range(N_HEADS)
          if h % 2 == par and (0 if (h // _SW_GROUP) == par else 1) == variant)
    for variant in range(2))
_SW_STACK_ROWS = len(_SW_STACKS[0]) * SW_BLOCK


def _swa_bias_init(relb_ref, bucket_ref, bias_ref):
    @pl.when((pl.program_id(0) == 0) & (pl.program_id(1) == 0))
    def _():
        bucket = bucket_ref[...]
        for h in range(N_HEADS):
            t = jnp.zeros(bucket.shape, jnp.float32)
            for bkt in range(N_BUCKETS):
                t = jnp.where(bucket == bkt, relb_ref[bkt, h], t)
            bias_ref[h] = t


def _swa_block(sb, sink_ref, q_ref, kvp_ref, kvc_ref, bias_ref, ob_ref):
    i = pl.program_id(1)
    head_lanes = _head_lanes()
    lane = lax.broadcasted_iota(jnp.int32, (SW_BLOCK, SW_BLOCK), 1)
    from_prev = lane > lax.broadcasted_iota(jnp.int32, (SW_BLOCK, SW_BLOCK), 0)
    lane1 = lax.broadcasted_iota(jnp.int32, (1, SW_BLOCK), 1)
    lead_pen = jnp.where((i == 0) & (lane1 < FIRST_VALID - (LEAD - SW_BLOCK)), -jnp.inf, 0.0)
    ones_cols = [jnp.broadcast_to(jnp.where(m, 1.0, 0.0).astype(jnp.bfloat16), (2 * SW_BLOCK, LANES))
                 for m in head_lanes]
    n_half = len(_SW_STACKS[0]) // 2
    rows = slice(sb * SW_BLOCK, (sb + 1) * SW_BLOCK)

    def kv_rows(blk):
        cur = kvc_ref[0, blk, :(sb + 1) * SW_BLOCK, :]
        if sb == 0:
            return jnp.concatenate([kvp_ref[0, blk], cur], axis=0)
        return cur[(sb - 1) * SW_BLOCK:]

    num = [None] * N_PAIRS
    den = [None] * N_PAIRS
    sink_term = [None] * N_HEADS
    for variant, heads in enumerate(_SW_STACKS):
        k2 = kv_rows(variant)
        v2 = kv_rows(2 + variant)
        qs = []
        for h in heads:
            q2 = q_ref[0, h // 2, rows, :]
            qs.append(jnp.where(head_lanes[h % 2], q2, jnp.zeros_like(q2)))
        logits = lax.dot_general(jnp.concatenate(qs, axis=0), k2, _NT,
                                 preferred_element_type=jnp.float32)
        weights = []
        for slot, h in enumerate(heads):
            lg = logits[slot * SW_BLOCK:(slot + 1) * SW_BLOCK]
            prev = lg[:, :SW_BLOCK]
            if sb == 0:
                prev = prev + lead_pen
            folded = jnp.where(from_prev, prev, lg[:, SW_BLOCK:]) + bias_ref[h]
            m = jnp.maximum(jnp.max(folded, axis=-1, keepdims=True), sink_ref[h])
            m = jnp.broadcast_to(m, folded.shape)
            e = jnp.exp(folded - m)
            sink_term[h] = jnp.exp(sink_ref[h] - m)
            zero = jnp.zeros_like(e)
            weights.append(jnp.concatenate(
                [jnp.where(from_prev, e, zero), jnp.where(from_prev, zero, e)],
                axis=1).astype(jnp.bfloat16))
        for par in range(2):
            vm = jnp.where(head_lanes[par], v2, jnp.zeros_like(v2))
            res = jnp.dot(jnp.concatenate(weights[par * n_half:(par + 1) * n_half], axis=0),
                          jnp.concatenate([vm, ones_cols[par]], axis=1),
                          preferred_element_type=jnp.float32)
            for s in range(n_half):
                p = heads[par * n_half + s] // 2
                part = res[s * SW_BLOCK:(s + 1) * SW_BLOCK]
                num[p] = part[:, :LANES] if num[p] is None else num[p] + part[:, :LANES]
                den[p] = part[:, LANES:] if den[p] is None else den[p] + part[:, LANES:]
    for p in range(N_PAIRS):
        denom = den[p] + jnp.where(head_lanes[0], sink_term[2 * p], sink_term[2 * p + 1])
        ob_ref[rows, p * LANES:(p + 1) * LANES] = num[p] / denom


def _mix_kernel(relb_ref, sink_ref, bucket_ref, qa_ref, ka_ref, va_ref, qb_ref, kvp_ref, kvc_ref,
                x_ref, nw_ref, wg0_ref, wg1_ref, wg2_ref, wg3_ref, wg4_ref, wb_ref, wo_ref, o_ref,
                bias_ref, xn_ref, act_ref, ob_ref, oa_ref, *sb_scratch):
    _swa_bias_init(relb_ref, bucket_ref, bias_ref)
    xn_ref[...] = _rms_rows(x_ref[0], nw_ref[...]).astype(jnp.bfloat16)
    chunk = W_CHUNK
    bw = BRANCH_WIDTH
    pieces = [(r,) + p for r, p in zip((wg0_ref, wg1_ref, wg2_ref, wg3_ref, wg4_ref), _GATE_PIECES)]

    def gate_chunk(c):
        def run():
            cols = slice(c * chunk, (c + 1) * chunk)
            a = jnp.dot(xn_ref[...], _w_chunk(pieces, _GATE_SRC_CHUNKS[c]),
                        preferred_element_type=jnp.float32)
            s = jax.nn.sigmoid(a)
            act_ref[:, cols] = a * s if c * chunk < 2 * bw else s
        return run

    _sb_tile(qa_ref, ka_ref, va_ref, oa_ref, *sb_scratch,
             fillers=[gate_chunk(c) for c in range(GATE_WIDTH // chunk)])
    for sb in range(TILE // SW_BLOCK):
        _swa_block(sb, sink_ref, qb_ref, kvp_ref, kvc_ref, bias_ref, ob_ref)
    oa = jnp.concatenate([oa_ref[p] for p in range(N_PAIRS)], axis=1)
    merged = None
    for g, o_branch in enumerate((oa, ob_ref[...])):
        u = (o_branch * act_ref[:, g * bw:(g + 1) * bw]).astype(jnp.bfloat16)
        y = jnp.dot(u, wb_ref[g], preferred_element_type=jnp.float32)
        term = act_ref[:, 2 * bw + g * D_MODEL:2 * bw + (g + 1) * D_MODEL] * y
        merged = term if merged is None else merged + term
    o_ref[0] = x_ref[0] + jnp.dot(merged.astype(jnp.bfloat16), wo_ref[...],
                                  preferred_element_type=jnp.float32)


def _mix(qkv, x, rel_bias, sinks, norm_w, w_in, w_branch, w_out):
    b, seq, d = x.shape
    assert all(len(s) == N_HEADS // 2 for s in _SW_STACKS)
    bucket = jnp.asarray(_t5_bucket_table())
    smem = pl.BlockSpec(memory_space=pltpu.SMEM)
    sub = TILE // SW_BLOCK
    lead_tiles = LEAD // TILE
    lp = seq + LEAD
    once = dict(pipeline_mode=pl.Buffered(1))
    const2 = lambda bi, i: (0, 0)
    tile_blocks = lambda blk: pl.BlockSpec((1, N_PAIRS, TILE, LANES),
                                           lambda bi, i: (bi, blk // N_PAIRS, i + lead_tiles, 0))
    all_rows = lambda blk: pl.BlockSpec((1, N_PAIRS, lp, LANES),
                                        lambda bi, i: (bi, blk // N_PAIRS, 0, 0), **once)
    return pl.pallas_call(
        _mix_kernel,
        grid=(b, seq // TILE),
        in_specs=[
            smem, smem,
            pl.BlockSpec((SW_BLOCK, SW_BLOCK), const2, **once),
            tile_blocks(BLK_QA), all_rows(BLK_KA), all_rows(BLK_VA),
            tile_blocks(BLK_QB),
            pl.BlockSpec((1, 4, SW_BLOCK, LANES),
                         lambda bi, i: (bi, BLK_KVB // 4, (i + lead_tiles) * sub - 1, 0)),
            pl.BlockSpec((1, 4, TILE, LANES), lambda bi, i: (bi, BLK_KVB // 4, i + lead_tiles, 0)),
            pl.BlockSpec((1, TILE, d), lambda bi, i: (bi, i, 0)),
            pl.BlockSpec((1, d), const2, **once),
            *[_w_piece_spec(d, *piece) for piece in _GATE_PIECES],
            pl.BlockSpec((2, BRANCH_WIDTH, d), lambda bi, i: (0, 0, 0), **once),
            pl.BlockSpec((d, d), const2, **once),
        ],
        out_specs=pl.BlockSpec((1, TILE, d), lambda bi, i: (bi, i, 0)),
        out_shape=jax.ShapeDtypeStruct((b, seq, d), jnp.float32),
        scratch_shapes=[pltpu.VMEM((N_HEADS, SW_BLOCK, SW_BLOCK), jnp.float32),
                        pltpu.VMEM((TILE, d), jnp.bfloat16),
                        pltpu.VMEM((TILE, GATE_WIDTH), jnp.float32),
                        pltpu.VMEM((TILE, BRANCH_WIDTH), jnp.float32),
                        ] + _SB_SCRATCH,
        compiler_params=pltpu.CompilerParams(
            dimension_semantics=("arbitrary", "arbitrary"), vmem_limit_bytes=VMEM_LIMIT),
        name="attend_mix",
    )(rel_bias, sinks, bucket, qkv, qkv, qkv, qkv, qkv, qkv, x, norm_w,
      *([w_in] * len(_GATE_PIECES)), w_branch, w_out)


def kernel(x, meta, rel_bias, norm_w, w_in, q_gain, k_gain, sinks, w_branch, w_out):
    b, seq, d = x.shape
    assert d == D_MODEL and norm_w.shape[0] == 1, "single-layer block of width 1024 only"
    assert meta.shape == (N_META, d) and seq % ROW_TILE == 0
    assert w_in.shape == (1, d, 4 * BRANCH_WIDTH + 2 * SW_KV_WIDTH + GATE_WIDTH)
    gq = jnp.tile(q_gain[0], 4)[None, :] * (HEAD_DIM ** -0.5)
    gk = jnp.tile(k_gain[0], 4)[None, :]
    nw = norm_w[0][None, :]

    qkv = _qkv_project(x, meta.astype(x.dtype), nw, w_in, gq, gk)
    return _mix(qkv, x, rel_bias, sinks[0], nw, w_in, w_branch[0].astype(jnp.bfloat16),
                w_out[0].astype(jnp.bfloat16))
```

```python
import math

import jax
import jax.numpy as jnp
import numpy as np
from jax import lax
from jax.experimental import pallas as pl
from jax.experimental.pallas import tpu as pltpu

D_MODEL = 1024
HEAD_DIM = 64
BRANCH_WIDTH = 512
SW_KV_WIDTH = 128
N_HEADS = BRANCH_WIDTH // HEAD_DIM
N_META = 16
WINDOW = 128
SW_BLOCK = 128
N_BUCKETS = 32
MAX_DISTANCE = 128
RMS_EPS = 1e-6

LANES = 128
TILE = 256
LEAD = 2 * TILE
FIRST_VALID = LEAD - N_META
ROW_TILE = 512
N_PAIRS = BRANCH_WIDTH // LANES
QKV_PROJ_WIDTH = 4 * BRANCH_WIDTH + 2 * SW_KV_WIDTH
QKV_BLOCKS = 4 * N_PAIRS + 4
BLK_QA, BLK_KA, BLK_VA, BLK_QB, BLK_KVB = 0, N_PAIRS, 2 * N_PAIRS, 3 * N_PAIRS, 4 * N_PAIRS
GATE_WIDTH = 2 * BRANCH_WIDTH + 2 * D_MODEL
W_CHUNK = 2 * LANES
_QKV_SRC_CHUNKS = (0, 1, 2, 3, 4, 5, 8, 9, 10)
_QKV_PIECES = ((0, 6), (8, 4))
_GATE_SRC_CHUNKS = (6, 7) + tuple(range(11, 21))
_GATE_PIECES = ((6, 2), (11, 1), (12, 4), (16, 4), (20, 1))
EXP_ZERO_BELOW = -105.0
VMEM_LIMIT = 56 * 1024 * 1024

_NT = (((1,), (1,)), ((), ()))


def _rms_rows(x, w):
    ms = jnp.mean(x * x, axis=-1, keepdims=True)
    return x * lax.rsqrt(ms + RMS_EPS) * w


def _exp_neg_abs(x):
    return jnp.exp2(jnp.abs(x) * (-math.log2(math.e)))


def _head_lanes():
    lane = lax.broadcasted_iota(jnp.int32, (1, LANES), 1)
    return lane < HEAD_DIM, lane >= HEAD_DIM


def _w_chunk(pieces, c):
    for ref, first, count in pieces:
        if first <= c < first + count:
            off = (c - first) * W_CHUNK
            return ref[0, :, off:off + W_CHUNK].astype(jnp.bfloat16)
    raise ValueError(c)


def _w_piece_spec(d, first, count):
    assert first % count == 0
    return pl.BlockSpec((1, d, count * W_CHUNK), lambda bi, i: (0, 0, first // count),
                        pipeline_mode=pl.Buffered(1))


def _qkv_kernel(x_ref, meta_ref, nw_ref, wa_ref, wb_ref, gq_ref, gk_ref, o_ref):
    i = pl.program_id(1)
    chunk = W_CHUNK
    pieces = [(wa_ref,) + _QKV_PIECES[0], (wb_ref,) + _QKV_PIECES[1]]
    r = lax.broadcasted_iota(jnp.int32, (chunk, chunk), 0) // HEAD_DIM
    c = lax.broadcasted_iota(jnp.int32, (chunk, chunk), 1) // HEAD_DIM
    head_ones = jnp.where(r == c, 1.0, 0.0).astype(jnp.bfloat16)

    def head_norm(a, gain):
        ssq = jnp.dot((a * a).astype(jnp.bfloat16), head_ones, preferred_element_type=jnp.float32)
        return a * lax.rsqrt(ssq * (1.0 / HEAD_DIM) + RMS_EPS) * gain

    def project(rows, r0, nrows):
        def emit(blk, a):
            o_ref[0, blk, r0:r0 + nrows, :] = a.astype(o_ref.dtype)

        xn = _rms_rows(rows, nw_ref[...]).astype(jnp.bfloat16)
        n_chunks = QKV_PROJ_WIDTH // chunk
        first_norm = BLK_QB // 2
        proj = lambda j: jnp.dot(xn, _w_chunk(pieces, _QKV_SRC_CHUNKS[j]),
                                 preferred_element_type=jnp.float32)
        to_norm = {j: proj(j) for j in range(first_norm, n_chunks)}
        for j in range(first_norm):
            a = proj(j)
            if 2 * j < BLK_KA:
                a = a * (HEAD_DIM ** -0.5)
            emit(2 * j, a[:, :LANES])
            emit(2 * j + 1, a[:, LANES:])
        for j, a in to_norm.items():
            if 2 * j < BLK_KVB:
                a = head_norm(a, gq_ref[...])
                emit(2 * j, a[:, :LANES])
                emit(2 * j + 1, a[:, LANES:])
            else:
                kb = head_norm(a, gk_ref[...])[:, :LANES]
                vb = a[:, LANES:]
                emit(BLK_KVB, kb)
                emit(BLK_KVB + 1, pltpu.roll(kb, HEAD_DIM, 1))
                emit(BLK_KVB + 2, vb)
                emit(BLK_KVB + 3, pltpu.roll(vb, HEAD_DIM, 1))

    @pl.when(i == 0)
    def _():
        o_ref[0, :, :FIRST_VALID, :] = jnp.zeros((QKV_BLOCKS, FIRST_VALID, LANES), o_ref.dtype)
        project(meta_ref[...], FIRST_VALID, N_META)

    @pl.when(i > 0)
    def _():
        project(x_ref[0], 0, ROW_TILE)


def _qkv_project(x, meta, norm_w, w_in, gq, gk):
    b, seq, d = x.shape
    assert LEAD == ROW_TILE and seq % ROW_TILE == 0
    n_tiles = (seq + LEAD) // ROW_TILE
    const = lambda bi, i: (0, 0)
    return pl.pallas_call(
        _qkv_kernel,
        grid=(b, n_tiles),
        in_specs=[
            pl.BlockSpec((1, ROW_TILE, d), lambda bi, i: (bi, jnp.maximum(i - 1, 0), 0)),
            pl.BlockSpec((N_META, d), const),
            pl.BlockSpec((1, d), const),
            _w_piece_spec(d, *_QKV_PIECES[0]),
            _w_piece_spec(d, *_QKV_PIECES[1]),
            pl.BlockSpec((1, 2 * LANES), const),
            pl.BlockSpec((1, 2 * LANES), const),
        ],
        out_specs=pl.BlockSpec((1, QKV_BLOCKS, ROW_TILE, LANES), lambda bi, i: (bi, 0, i, 0)),
        out_shape=jax.ShapeDtypeStruct((b, QKV_BLOCKS, seq + LEAD, LANES), jnp.bfloat16),
        compiler_params=pltpu.CompilerParams(
            dimension_semantics=("arbitrary", "arbitrary"), vmem_limit_bytes=VMEM_LIMIT),
        name="qkv_project",
    )(x, meta, norm_w, w_in, w_in, gq, gk)


def _sb_tile(q_ref, k_ref, v_ref, acc_ref, carry_ref, negl_ref, logsig_ref, first_ref, w_ref,
             fillers=()):
    pending = list(fillers)
    qt = pl.program_id(1) + LEAD // TILE
    head_lanes = _head_lanes()
    row = lax.broadcasted_iota(jnp.int32, (TILE, TILE), 0)
    col = lax.broadcasted_iota(jnp.int32, (TILE, TILE), 1)
    later_sum = jnp.where(row > col, 1.0, 0.0).astype(jnp.bfloat16)
    masks = {"diag": row > col, "lead": col >= FIRST_VALID - TILE, "full": None,
             "prev": col >= FIRST_VALID - (qt - 1) * TILE}

    def all_pairs(j, mode, fill=False):
        mask = masks[mode]
        start = pl.multiple_of(j * TILE, TILE)
        for p in range(N_PAIRS):
            q2 = q_ref[0, p]
            k2 = k_ref[0, p, pl.ds(start, TILE), :]
            for h in range(2):
                qm = jnp.where(head_lanes[h], q2, jnp.zeros_like(q2))
                z = lax.dot_general(qm, k2, _NT, preferred_element_type=jnp.float32)
                neg_l = jnp.maximum(z, 0.0) + jnp.log(1.0 + _exp_neg_abs(z))
                if mask is not None:
                    neg_l = jnp.where(mask, neg_l, 0.0)
                negl_ref[p, h] = neg_l.astype(jnp.bfloat16)
                logsig_ref[p, h] = z - neg_l
                first_ref[p, h] = neg_l[:, :LANES]
                if fill and pending:
                    pending.pop(0)()
        for p in range(N_PAIRS):
            for h in range(2):
                later = jnp.dot(negl_ref[p, h], later_sum, preferred_element_type=jnp.float32)
                total = jnp.broadcast_to(later[:, :1] + first_ref[p, h][:, :1], (TILE, LANES))
                logw = logsig_ref[p, h] - later
                if mode != "diag":
                    carry = carry_ref[p, h]
                    logw = logw - jnp.concatenate([carry] * (TILE // LANES), axis=1)
                    carry_ref[p, h] = carry + total
                else:
                    carry_ref[p, h] = total
                w = jnp.exp(logw)
                if mask is not None:
                    w = jnp.where(mask, w, 0.0)
                w_ref[p, h] = w.astype(jnp.bfloat16)
        for p in range(N_PAIRS):
            v2 = v_ref[0, p, pl.ds(start, TILE), :]
            upd = None
            for h in range(2):
                vm = jnp.where(head_lanes[h], v2, jnp.zeros_like(v2))
                pv = jnp.dot(w_ref[p, h], vm, preferred_element_type=jnp.float32)
                upd = pv if upd is None else upd + pv
            if mode == "diag":
                acc_ref[p] = upd
            else:
                acc_ref[p] += upd

    def still_visible():
        return jnp.min(carry_ref[...]) < -EXP_ZERO_BELOW

    first_full = LEAD // TILE
    all_pairs(qt, "diag", fill=True)
    all_pairs(qt - 1, "prev", fill=True)
    while pending:
        pending.pop(0)()

    more = still_visible() & (qt > first_full)

    def walk_rest():
        def cond(state):
            j, go = state
            return (j >= first_full) & go

        def body(state):
            j, _ = state
            all_pairs(j, "full")
            return j - 1, still_visible()

        j_end, go = lax.while_loop(cond, body, (qt - 2, more))

        @pl.when(go)
        def _():
            all_pairs(first_full - 1, "lead")

    return more, walk_rest


_SB_SCRATCH = [pltpu.VMEM((N_PAIRS, TILE, LANES), jnp.float32),
               pltpu.VMEM((N_PAIRS, 2, TILE, LANES), jnp.float32),
               pltpu.VMEM((N_PAIRS, 2, TILE, TILE), jnp.bfloat16),
               pltpu.VMEM((N_PAIRS, 2, TILE, TILE), jnp.float32),
               pltpu.VMEM((N_PAIRS, 2, TILE, LANES), jnp.float32),
               pltpu.VMEM((N_PAIRS, 2, TILE, TILE), jnp.bfloat16)]


def _t5_bucket_table():
    r = np.arange(SW_BLOCK)[:, None]
    l = np.arange(SW_BLOCK)[None, :]
    n = np.where(l > r, SW_BLOCK + r - l, r - l)
    assert WINDOW == SW_BLOCK and n.min() >= 0 and n.max() < WINDOW
    max_exact = N_BUCKETS // 2
    large = max_exact + (np.log(np.maximum(n, 1) / max_exact)
                         / math.log(MAX_DISTANCE / max_exact)
                         * (N_BUCKETS - max_exact)).astype(np.int32)
    large = np.minimum(large, N_BUCKETS - 1)
    return np.where(n < max_exact, n, large).astype(np.int32)


_SW_GROUP = N_HEADS // (SW_KV_WIDTH // HEAD_DIM)
_SW_STACKS = tuple(
    tuple(h for par in range(2) for h in range(N_HEADS)
          if h % 2 == par and (0 if (h // _SW_GROUP) == par else 1) == variant)
    for variant in range(2))
_SW_STACK_ROWS = len(_SW_STACKS[0]) * SW_BLOCK


def _swa_bias_init(relb_ref, bucket_ref, bias_ref):
    @pl.when((pl.program_id(0) == 0) & (pl.program_id(1) == 0))
    def _():
        bucket = bucket_ref[...]
        for h in range(N_HEADS):
            t = jnp.zeros(bucket.shape, jnp.float32)
            for bkt in range(N_BUCKETS):
                t = jnp.where(bucket == bkt, relb_ref[bkt, h], t)
            bias_ref[h] = t


def _swa_block(sb, sink_ref, q_ref, kvp_ref, kvc_ref, bias_ref, ob_ref):
    i = pl.program_id(1)
    head_lanes = _head_lanes()
    lane = lax.broadcasted_iota(jnp.int32, (SW_BLOCK, SW_BLOCK), 1)
    from_prev = lane > lax.broadcasted_iota(jnp.int32, (SW_BLOCK, SW_BLOCK), 0)
    lane1 = lax.broadcasted_iota(jnp.int32, (1, SW_BLOCK), 1)
    lead_pen = jnp.where((i == 0) & (lane1 < FIRST_VALID - (LEAD - SW_BLOCK)), -jnp.inf, 0.0)
    ones_cols = [jnp.broadcast_to(jnp.where(m, 1.0, 0.0).astype(jnp.bfloat16), (2 * SW_BLOCK, LANES))
                 for m in head_lanes]
    n_half = len(_SW_STACKS[0]) // 2
    rows = slice(sb * SW_BLOCK, (sb + 1) * SW_BLOCK)

    def kv_rows(blk):
        cur = kvc_ref[0, blk, :(sb + 1) * SW_BLOCK, :]
        if sb == 0:
            return jnp.concatenate([kvp_ref[0, blk], cur], axis=0)
        return cur[(sb - 1) * SW_BLOCK:]

    num = [None] * N_PAIRS
    den = [None] * N_PAIRS
    sink_term = [None] * N_HEADS
    for variant, heads in enumerate(_SW_STACKS):
        k2 = kv_rows(variant)
        v2 = kv_rows(2 + variant)
        qs = []
        for h in heads:
            q2 = q_ref[0, h // 2, rows, :]
            qs.append(jnp.where(head_lanes[h % 2], q2, jnp.zeros_like(q2)))
        logits = lax.dot_general(jnp.concatenate(qs, axis=0), k2, _NT,
                                 preferred_element_type=jnp.float32)
        weights = []
        for slot, h in enumerate(heads):
            lg = logits[slot * SW_BLOCK:(slot + 1) * SW_BLOCK]
            prev = lg[:, :SW_BLOCK]
            if sb == 0:
                prev = prev + lead_pen
            folded = jnp.where(from_prev, prev, lg[:, SW_BLOCK:]) + bias_ref[h]
            m = jnp.maximum(jnp.max(folded, axis=-1, keepdims=True), sink_ref[h])
            m = jnp.broadcast_to(m, folded.shape)
            e = jnp.exp(folded - m)
            sink_term[h] = jnp.exp(sink_ref[h] - m)
            zero = jnp.zeros_like(e)
            weights.append(jnp.concatenate(
                [jnp.where(from_prev, e, zero), jnp.where(from_prev, zero, e)],
                axis=1).astype(jnp.bfloat16))
        for par in range(2):
            vm = jnp.where(head_lanes[par], v2, jnp.zeros_like(v2))
            res = jnp.dot(jnp.concatenate(weights[par * n_half:(par + 1) * n_half], axis=0),
                          jnp.concatenate([vm, ones_cols[par]], axis=1),
                          preferred_element_type=jnp.float32)
            for s in range(n_half):
                p = heads[par * n_half + s] // 2
                part = res[s * SW_BLOCK:(s + 1) * SW_BLOCK]
                num[p] = part[:, :LANES] if num[p] is None else num[p] + part[:, :LANES]
                den[p] = part[:, LANES:] if den[p] is None else den[p] + part[:, LANES:]
    for p in range(N_PAIRS):
        denom = den[p] + jnp.where(head_lanes[0], sink_term[2 * p], sink_term[2 * p + 1])
        ob_ref[rows, p * LANES:(p + 1) * LANES] = num[p] / denom


def _mix_kernel(relb_ref, sink_ref, bucket_ref, tile_ref, ka_lead_ref, va_lead_ref, kvp_ref,
                x_ref, nw_ref, wg0_ref, wg1_ref, wg2_ref, wg3_ref, wg4_ref, wb_ref, wo_ref, o_ref,
                bias_ref, xn_ref, act_ref, ob_ref, ka_ref, va_ref, oa_ref, *sb_scratch):
    i = pl.program_id(1)
    lead_tile = LEAD // TILE - 1
    qa_ref, ka_tile_ref, va_tile_ref, qb_ref = (
        tile_ref.at[:, blk:blk + N_PAIRS] for blk in (BLK_QA, BLK_KA, BLK_VA, BLK_QB))
    kvc_ref = tile_ref.at[:, BLK_KVB:BLK_KVB + 4]

    @pl.when(i == 0)
    def _():
        ka_ref[0, :, lead_tile * TILE:LEAD, :] = ka_lead_ref[0]
        va_ref[0, :, lead_tile * TILE:LEAD, :] = va_lead_ref[0]

    own_rows = pl.ds(pl.multiple_of((i + LEAD // TILE) * TILE, TILE), TILE)
    ka_ref[0, :, own_rows, :] = ka_tile_ref[0]
    va_ref[0, :, own_rows, :] = va_tile_ref[0]
    _swa_bias_init(relb_ref, bucket_ref, bias_ref)
    xn_ref[...] = _rms_rows(x_ref[0], nw_ref[...]).astype(jnp.bfloat16)
    chunk = W_CHUNK
    bw = BRANCH_WIDTH
    pieces = [(r,) + p for r, p in zip((wg0_ref, wg1_ref, wg2_ref, wg3_ref, wg4_ref), _GATE_PIECES)]

    def gate_chunk(c):
        def run():
            cols = slice(c * chunk, (c + 1) * chunk)
            a = jnp.dot(xn_ref[...], _w_chunk(pieces, _GATE_SRC_CHUNKS[c]),
                        preferred_element_type=jnp.float32)
            s = jax.nn.sigmoid(a)
            act_ref[:, cols] = a * s if c * chunk < 2 * bw else s
        return run

    more, walk_rest = _sb_tile(qa_ref, ka_ref, va_ref, oa_ref, *sb_scratch,
                               fillers=[gate_chunk(c) for c in range(GATE_WIDTH // chunk)])
    for sb in range(TILE // SW_BLOCK):
        _swa_block(sb, sink_ref, qb_ref, kvp_ref, kvc_ref, bias_ref, ob_ref)

    def mix_out():
        oa = jnp.concatenate([oa_ref[p] for p in range(N_PAIRS)], axis=1)
        merged = None
        for g, o_branch in enumerate((oa, ob_ref[...])):
            u = (o_branch * act_ref[:, g * bw:(g + 1) * bw]).astype(jnp.bfloat16)
            y = jnp.dot(u, wb_ref[g], preferred_element_type=jnp.float32)
            term = act_ref[:, 2 * bw + g * D_MODEL:2 * bw + (g + 1) * D_MODEL] * y
            merged = term if merged is None else merged + term
        o_ref[0] = x_ref[0] + jnp.dot(merged.astype(jnp.bfloat16), wo_ref[...],
                                      preferred_element_type=jnp.float32)

    mix_out()

    @pl.when(more)
    def _():
        walk_rest()
        mix_out()


def _mix(qkv, x, rel_bias, sinks, norm_w, w_in, w_branch, w_out):
    b, seq, d = x.shape
    assert all(len(s) == N_HEADS // 2 for s in _SW_STACKS)
    bucket = jnp.asarray(_t5_bucket_table())
    smem = pl.BlockSpec(memory_space=pltpu.SMEM)
    sub = TILE // SW_BLOCK
    lead_tiles = LEAD // TILE
    lp = seq + LEAD
    once = dict(pipeline_mode=pl.Buffered(1))
    const2 = lambda bi, i: (0, 0)
    lead_blocks = lambda blk: pl.BlockSpec((1, N_PAIRS, TILE, LANES),
                                           lambda bi, i: (bi, blk // N_PAIRS, lead_tiles - 1, 0))
    return pl.pallas_call(
        _mix_kernel,
        grid=(b, seq // TILE),
        in_specs=[
            smem, smem,
            pl.BlockSpec((SW_BLOCK, SW_BLOCK), const2, **once),
            pl.BlockSpec((1, QKV_BLOCKS, TILE, LANES), lambda bi, i: (bi, 0, i + lead_tiles, 0)),
            lead_blocks(BLK_KA), lead_blocks(BLK_VA),
            pl.BlockSpec((1, 4, SW_BLOCK, LANES),
                         lambda bi, i: (bi, BLK_KVB // 4, (i + lead_tiles) * sub - 1, 0)),
            pl.BlockSpec((1, TILE, d), lambda bi, i: (bi, i, 0)),
            pl.BlockSpec((1, d), const2, **once),
            *[_w_piece_spec(d, *piece) for piece in _GATE_PIECES],
            pl.BlockSpec((2, BRANCH_WIDTH, d), lambda bi, i: (0, 0, 0), **once),
            pl.BlockSpec((d, d), const2, **once),
        ],
        out_specs=pl.BlockSpec((1, TILE, d), lambda bi, i: (bi, i, 0)),
        out_shape=jax.ShapeDtypeStruct((b, seq, d), jnp.float32),
        scratch_shapes=[pltpu.VMEM((N_HEADS, SW_BLOCK, SW_BLOCK), jnp.float32),
                        pltpu.VMEM((TILE, d), jnp.bfloat16),
                        pltpu.VMEM((TILE, GATE_WIDTH), jnp.float32),
                        pltpu.VMEM((TILE, BRANCH_WIDTH), jnp.float32),
                        pltpu.VMEM((1, N_PAIRS, lp, LANES), jnp.bfloat16),
                        pltpu.VMEM((1, N_PAIRS, lp, LANES), jnp.bfloat16),
                        ] + _SB_SCRATCH,
        compiler_params=pltpu.CompilerParams(
            dimension_semantics=("arbitrary", "arbitrary"), vmem_limit_bytes=VMEM_LIMIT),
        name="attend_mix",
    )(rel_bias, sinks, bucket, *([qkv] * 4), x, norm_w,
      *([w_in] * len(_GATE_PIECES)), w_branch, w_out)


def kernel(x, meta, rel_bias, norm_w, w_in, q_gain, k_gain, sinks, w_branch, w_out):
    b, seq, d = x.shape
    assert d == D_MODEL and norm_w.shape[0] == 1, "single-layer block of width 1024 only"
    assert meta.shape == (N_META, d) and seq % ROW_TILE == 0
    assert w_in.shape == (1, d, 4 * BRANCH_WIDTH + 2 * SW_KV_WIDTH + GATE_WIDTH)
    gq = jnp.tile(q_gain[0], 4)[None, :] * (HEAD_DIM ** -0.5)
    gk = jnp.tile(k_gain[0], 4)[None, :]
    nw = norm_w[0][None, :]

    qkv = _qkv_project(x, meta.astype(x.dtype), nw, w_in, gq, gk)
    return _mix(qkv, x, rel_bias, sinks[0], nw, w_in, w_branch[0].astype(jnp.bfloat16),
                w_out[0].astype(jnp.bfloat16))
```

```python
import math

import jax
import jax.numpy as jnp
import numpy as np
from jax import lax
from jax.experimental import pallas as pl
from jax.experimental.pallas import tpu as pltpu

D_MODEL = 1024
HEAD_DIM = 64
BRANCH_WIDTH = 512
SW_KV_WIDTH = 128
N_HEADS = BRANCH_WIDTH // HEAD_DIM
N_META = 16
WINDOW = 128
SW_BLOCK = 128
N_BUCKETS = 32
MAX_DISTANCE = 128
RMS_EPS = 1e-6

LANES = 128
TILE = 256
LEAD = 2 * TILE
FIRST_VALID = LEAD - N_META
ROW_TILE = 512
N_PAIRS = BRANCH_WIDTH // LANES
QKV_PROJ_WIDTH = 4 * BRANCH_WIDTH + 2 * SW_KV_WIDTH
QKV_BLOCKS = 4 * N_PAIRS + 4
BLK_QA, BLK_KA, BLK_VA, BLK_QB, BLK_KVB = 0, N_PAIRS, 2 * N_PAIRS, 3 * N_PAIRS, 4 * N_PAIRS
GATE_WIDTH = 2 * BRANCH_WIDTH + 2 * D_MODEL
W_CHUNK = 2 * LANES
_QKV_SRC_CHUNKS = (0, 1, 2, 3, 4, 5, 8, 9, 10)
_QKV_PIECES = ((0, 6), (8, 4))
_GATE_SRC_CHUNKS = (6, 7) + tuple(range(11, 21))
_GATE_PIECES = ((6, 2), (11, 1), (12, 4), (16, 4), (20, 1))
EXP_ZERO_BELOW = -105.0
VMEM_LIMIT = 56 * 1024 * 1024

_NT = (((1,), (1,)), ((), ()))


def _rms_rows(x, w):
    ms = jnp.mean(x * x, axis=-1, keepdims=True)
    return x * lax.rsqrt(ms + RMS_EPS) * w


def _exp_neg_abs(x):
    return jnp.exp2(jnp.abs(x) * (-math.log2(math.e)))


def _head_lanes():
    lane = lax.broadcasted_iota(jnp.int32, (1, LANES), 1)
    return lane < HEAD_DIM, lane >= HEAD_DIM


def _w_chunk(pieces, c):
    for ref, first, count in pieces:
        if first <= c < first + count:
            off = (c - first) * W_CHUNK
            return ref[0, :, off:off + W_CHUNK].astype(jnp.bfloat16)
    raise ValueError(c)


def _w_piece_spec(d, first, count):
    assert first % count == 0
    return pl.BlockSpec((1, d, count * W_CHUNK), lambda bi, i: (0, 0, first // count),
                        pipeline_mode=pl.Buffered(1))


def _qkv_kernel(x_ref, meta_ref, nw_ref, wa_ref, wb_ref, gq_ref, gk_ref, o_ref):
    i = pl.program_id(1)
    chunk = W_CHUNK
    pieces = [(wa_ref,) + _QKV_PIECES[0], (wb_ref,) + _QKV_PIECES[1]]
    r = lax.broadcasted_iota(jnp.int32, (chunk, chunk), 0) // HEAD_DIM
    c = lax.broadcasted_iota(jnp.int32, (chunk, chunk), 1) // HEAD_DIM
    head_ones = jnp.where(r == c, 1.0, 0.0).astype(jnp.bfloat16)

    def head_norm(a, gain):
        ssq = jnp.dot((a * a).astype(jnp.bfloat16), head_ones, preferred_element_type=jnp.float32)
        return a * lax.rsqrt(ssq * (1.0 / HEAD_DIM) + RMS_EPS) * gain

    def project(rows, r0, nrows):
        def emit(blk, a):
            o_ref[0, blk, r0:r0 + nrows, :] = a.astype(o_ref.dtype)

        xn = _rms_rows(rows, nw_ref[...]).astype(jnp.bfloat16)
        n_chunks = QKV_PROJ_WIDTH // chunk
        first_norm = BLK_QB // 2
        proj = lambda j: jnp.dot(xn, _w_chunk(pieces, _QKV_SRC_CHUNKS[j]),
                                 preferred_element_type=jnp.float32)
        to_norm = {j: proj(j) for j in range(first_norm, n_chunks)}
        for j in range(first_norm):
            a = proj(j)
            if 2 * j < BLK_KA:
                a = a * (HEAD_DIM ** -0.5)
            emit(2 * j, a[:, :LANES])
            emit(2 * j + 1, a[:, LANES:])
        for j, a in to_norm.items():
            if 2 * j < BLK_KVB:
                a = head_norm(a, gq_ref[...])
                emit(2 * j, a[:, :LANES])
                emit(2 * j + 1, a[:, LANES:])
            else:
                kb = head_norm(a, gk_ref[...])[:, :LANES]
                vb = a[:, LANES:]
                emit(BLK_KVB, kb)
                emit(BLK_KVB + 1, pltpu.roll(kb, HEAD_DIM, 1))
                emit(BLK_KVB + 2, vb)
                emit(BLK_KVB + 3, pltpu.roll(vb, HEAD_DIM, 1))

    @pl.when(i == 0)
    def _():
        o_ref[0, :, :FIRST_VALID, :] = jnp.zeros((QKV_BLOCKS, FIRST_VALID, LANES), o_ref.dtype)
        project(meta_ref[...], FIRST_VALID, N_META)

    @pl.when(i > 0)
    def _():
        project(x_ref[0], 0, ROW_TILE)


def _qkv_project(x, meta, norm_w, w_in, gq, gk):
    b, seq, d = x.shape
    assert LEAD == ROW_TILE and seq % ROW_TILE == 0
    n_tiles = (seq + LEAD) // ROW_TILE
    const = lambda bi, i: (0, 0)
    return pl.pallas_call(
        _qkv_kernel,
        grid=(b, n_tiles),
        in_specs=[
            pl.BlockSpec((1, ROW_TILE, d), lambda bi, i: (bi, jnp.maximum(i - 1, 0), 0)),
            pl.BlockSpec((N_META, d), const),
            pl.BlockSpec((1, d), const),
            _w_piece_spec(d, *_QKV_PIECES[0]),
            _w_piece_spec(d, *_QKV_PIECES[1]),
            pl.BlockSpec((1, 2 * LANES), const),
            pl.BlockSpec((1, 2 * LANES), const),
        ],
        out_specs=pl.BlockSpec((1, QKV_BLOCKS, ROW_TILE, LANES), lambda bi, i: (bi, 0, i, 0)),
        out_shape=jax.ShapeDtypeStruct((b, QKV_BLOCKS, seq + LEAD, LANES), jnp.bfloat16),
        compiler_params=pltpu.CompilerParams(
            dimension_semantics=("arbitrary", "arbitrary"), vmem_limit_bytes=VMEM_LIMIT),
        name="qkv_project",
    )(x, meta, norm_w, w_in, w_in, gq, gk)


def _sb_tile(q_ref, k_ref, v_ref, acc_ref, carry_ref, negl_ref, logsig_ref, first_ref, w_ref,
             fillers=()):
    pending = list(fillers)
    qt = pl.program_id(1) + LEAD // TILE
    head_lanes = _head_lanes()
    row = lax.broadcasted_iota(jnp.int32, (TILE, TILE), 0)
    col = lax.broadcasted_iota(jnp.int32, (TILE, TILE), 1)
    later_sum = jnp.where(row > col, 1.0, 0.0).astype(jnp.bfloat16)
    masks = {"diag": row > col, "lead": col >= FIRST_VALID - TILE, "full": None,
             "prev": col >= FIRST_VALID - (qt - 1) * TILE}

    def over_pairs(one_pair, rolled):
        if rolled:
            lax.fori_loop(0, N_PAIRS, lambda p, c: (one_pair(p), c)[1], 0)
        else:
            for p in range(N_PAIRS):
                one_pair(p)

    def all_pairs(j, mode, fill=False):
        mask = masks[mode]
        rolled = not fill
        start = pl.multiple_of(j * TILE, TILE)

        def scores(p):
            q2 = q_ref[0, p]
            k2 = k_ref[0, p, pl.ds(start, TILE), :]
            for h in range(2):
                qm = jnp.where(head_lanes[h], q2, jnp.zeros_like(q2))
                z = lax.dot_general(qm, k2, _NT, preferred_element_type=jnp.float32)
                neg_l = jnp.maximum(z, 0.0) + jnp.log(1.0 + _exp_neg_abs(z))
                if mask is not None:
                    neg_l = jnp.where(mask, neg_l, 0.0)
                negl_ref[p, h] = neg_l.astype(jnp.bfloat16)
                logsig_ref[p, h] = z - neg_l
                first_ref[p, h] = neg_l[:, :LANES]
                if fill and pending:
                    pending.pop(0)()

        def weights(p):
            for h in range(2):
                later = jnp.dot(negl_ref[p, h], later_sum, preferred_element_type=jnp.float32)
                total = jnp.broadcast_to(later[:, :1] + first_ref[p, h][:, :1], (TILE, LANES))
                logw = logsig_ref[p, h] - later
                if mode != "diag":
                    carry = carry_ref[p, h]
                    logw = logw - jnp.concatenate([carry] * (TILE // LANES), axis=1)
                    carry_ref[p, h] = carry + total
                else:
                    carry_ref[p, h] = total
                w = jnp.exp(logw)
                if mask is not None:
                    w = jnp.where(mask, w, 0.0)
                w_ref[p, h] = w.astype(jnp.bfloat16)

        def values(p):
            v2 = v_ref[0, p, pl.ds(start, TILE), :]
            upd = None
            for h in range(2):
                vm = jnp.where(head_lanes[h], v2, jnp.zeros_like(v2))
                pv = jnp.dot(w_ref[p, h], vm, preferred_element_type=jnp.float32)
                upd = pv if upd is None else upd + pv
            if mode == "diag":
                acc_ref[p] = upd
            else:
                acc_ref[p] += upd

        for one_pair in (scores, weights, values):
            over_pairs(one_pair, rolled)

    def still_visible():
        return jnp.min(carry_ref[...]) < -EXP_ZERO_BELOW

    first_full = LEAD // TILE
    all_pairs(qt, "diag", fill=True)
    all_pairs(qt - 1, "prev", fill=True)
    while pending:
        pending.pop(0)()

    def cond(state):
        j, go = state
        return (j >= first_full) & go

    def body(state):
        j, _ = state
        all_pairs(j, "full")
        return j - 1, still_visible()

    j_end, go = lax.while_loop(cond, body, (qt - 2, still_visible()))

    @pl.when(go & (qt > first_full))
    def _():
        all_pairs(first_full - 1, "lead")


_SB_SCRATCH = [pltpu.VMEM((N_PAIRS, TILE, LANES), jnp.float32),
               pltpu.VMEM((N_PAIRS, 2, TILE, LANES), jnp.float32),
               pltpu.VMEM((N_PAIRS, 2, TILE, TILE), jnp.bfloat16),
               pltpu.VMEM((N_PAIRS, 2, TILE, TILE), jnp.float32),
               pltpu.VMEM((N_PAIRS, 2, TILE, LANES), jnp.float32),
               pltpu.VMEM((N_PAIRS, 2, TILE, TILE), jnp.bfloat16)]


def _t5_bucket_table():
    r = np.arange(SW_BLOCK)[:, None]
    l = np.arange(SW_BLOCK)[None, :]
    n = np.where(l > r, SW_BLOCK + r - l, r - l)
    assert WINDOW == SW_BLOCK and n.min() >= 0 and n.max() < WINDOW
    max_exact = N_BUCKETS // 2
    large = max_exact + (np.log(np.maximum(n, 1) / max_exact)
                         / math.log(MAX_DISTANCE / max_exact)
                         * (N_BUCKETS - max_exact)).astype(np.int32)
    large = np.minimum(large, N_BUCKETS - 1)
    return np.where(n < max_exact, n, large).astype(np.int32)


_SW_GROUP = N_HEADS // (SW_KV_WIDTH // HEAD_DIM)
_SW_STACKS = tuple(
    tuple(h for par in range(2) for h in range(N_HEADS)
          if h % 2 == par and (0 if (h // _SW_GROUP) == par else 1) == variant)
    for variant in range(2))
_SW_STACK_ROWS = len(_SW_STACKS[0]) * SW_BLOCK


def _swa_bias_init(relb_ref, bucket_ref, bias_ref):
    @pl.when((pl.program_id(0) == 0) & (pl.program_id(1) == 0))
    def _():
        bucket = bucket_ref[...]
        for h in range(N_HEADS):
            t = jnp.zeros(bucket.shape, jnp.float32)
            for bkt in range(N_BUCKETS):
                t = jnp.where(bucket == bkt, relb_ref[bkt, h], t)
            bias_ref[h] = t


def _swa_block(sb, sink_ref, q_ref, kvp_ref, kvc_ref, bias_ref, ob_ref):
    i = pl.program_id(1)
    head_lanes = _head_lanes()
    lane = lax.broadcasted_iota(jnp.int32, (SW_BLOCK, SW_BLOCK), 1)
    from_prev = lane > lax.broadcasted_iota(jnp.int32, (SW_BLOCK, SW_BLOCK), 0)
    lane1 = lax.broadcasted_iota(jnp.int32, (1, SW_BLOCK), 1)
    lead_pen = jnp.where((i == 0) & (lane1 < FIRST_VALID - (LEAD - SW_BLOCK)), -jnp.inf, 0.0)
    ones_cols = [jnp.broadcast_to(jnp.where(m, 1.0, 0.0).astype(jnp.bfloat16), (2 * SW_BLOCK, LANES))
                 for m in head_lanes]
    n_half = len(_SW_STACKS[0]) // 2
    rows = slice(sb * SW_BLOCK, (sb + 1) * SW_BLOCK)

    def kv_rows(blk):
        cur = kvc_ref[0, blk, :(sb + 1) * SW_BLOCK, :]
        if sb == 0:
            return jnp.concatenate([kvp_ref[0, blk], cur], axis=0)
        return cur[(sb - 1) * SW_BLOCK:]

    num = [None] * N_PAIRS
    den = [None] * N_PAIRS
    sink_term = [None] * N_HEADS
    for variant, heads in enumerate(_SW_STACKS):
        k2 = kv_rows(variant)
        v2 = kv_rows(2 + variant)
        qs = []
        for h in heads:
            q2 = q_ref[0, h // 2, rows, :]
            qs.append(jnp.where(head_lanes[h % 2], q2, jnp.zeros_like(q2)))
        logits = lax.dot_general(jnp.concatenate(qs, axis=0), k2, _NT,
                                 preferred_element_type=jnp.float32)
        weights = []
        for slot, h in enumerate(heads):
            lg = logits[slot * SW_BLOCK:(slot + 1) * SW_BLOCK]
            prev = lg[:, :SW_BLOCK]
            if sb == 0:
                prev = prev + lead_pen
            folded = jnp.where(from_prev, prev, lg[:, SW_BLOCK:]) + bias_ref[h]
            m = jnp.maximum(jnp.max(folded, axis=-1, keepdims=True), sink_ref[h])
            m = jnp.broadcast_to(m, folded.shape)
            e = jnp.exp(folded - m)
            sink_term[h] = jnp.exp(sink_ref[h] - m)
            zero = jnp.zeros_like(e)
            weights.append(jnp.concatenate(
                [jnp.where(from_prev, e, zero), jnp.where(from_prev, zero, e)],
                axis=1).astype(jnp.bfloat16))
        for par in range(2):
            vm = jnp.where(head_lanes[par], v2, jnp.zeros_like(v2))
            res = jnp.dot(jnp.concatenate(weights[par * n_half:(par + 1) * n_half], axis=0),
                          jnp.concatenate([vm, ones_cols[par]], axis=1),
                          preferred_element_type=jnp.float32)
            for s in range(n_half):
                p = heads[par * n_half + s] // 2
                part = res[s * SW_BLOCK:(s + 1) * SW_BLOCK]
                num[p] = part[:, :LANES] if num[p] is None else num[p] + part[:, :LANES]
                den[p] = part[:, LANES:] if den[p] is None else den[p] + part[:, LANES:]
    for p in range(N_PAIRS):
        denom = den[p] + jnp.where(head_lanes[0], sink_term[2 * p], sink_term[2 * p + 1])
        ob_ref[rows, p * LANES:(p + 1) * LANES] = num[p] / denom


def _mix_kernel(relb_ref, sink_ref, bucket_ref, tile_ref, ka_lead_ref, va_lead_ref, kvp_ref,
                x_ref, nw_ref, wg0_ref, wg1_ref, wg2_ref, wg3_ref, wg4_ref, wb_ref, wo_ref, o_ref,
                bias_ref, xn_ref, act_ref, ob_ref, ka_ref, va_ref, oa_ref, *sb_scratch):
    i = pl.program_id(1)
    lead_tile = LEAD // TILE - 1
    qa_ref, ka_tile_ref, va_tile_ref, qb_ref = (
        tile_ref.at[:, blk:blk + N_PAIRS] for blk in (BLK_QA, BLK_KA, BLK_VA, BLK_QB))
    kvc_ref = tile_ref.at[:, BLK_KVB:BLK_KVB + 4]

    @pl.when(i == 0)
    def _():
        ka_ref[0, :, lead_tile * TILE:LEAD, :] = ka_lead_ref[0]
        va_ref[0, :, lead_tile * TILE:LEAD, :] = va_lead_ref[0]

    own_rows = pl.ds(pl.multiple_of((i + LEAD // TILE) * TILE, TILE), TILE)
    ka_ref[0, :, own_rows, :] = ka_tile_ref[0]
    va_ref[0, :, own_rows, :] = va_tile_ref[0]
    _swa_bias_init(relb_ref, bucket_ref, bias_ref)
    xn_ref[...] = _rms_rows(x_ref[0], nw_ref[...]).astype(jnp.bfloat16)
    chunk = W_CHUNK
    bw = BRANCH_WIDTH
    pieces = [(r,) + p for r, p in zip((wg0_ref, wg1_ref, wg2_ref, wg3_ref, wg4_ref), _GATE_PIECES)]

    def gate_chunk(c):
        def run():
            cols = slice(c * chunk, (c + 1) * chunk)
            a = jnp.dot(xn_ref[...], _w_chunk(pieces, _GATE_SRC_CHUNKS[c]),
                        preferred_element_type=jnp.float32)
            s = jax.nn.sigmoid(a)
            act_ref[:, cols] = a * s if c * chunk < 2 * bw else s
        return run

    _sb_tile(qa_ref, ka_ref, va_ref, oa_ref, *sb_scratch,
             fillers=[gate_chunk(c) for c in range(GATE_WIDTH // chunk)])
    for sb in range(TILE // SW_BLOCK):
        _swa_block(sb, sink_ref, qb_ref, kvp_ref, kvc_ref, bias_ref, ob_ref)
    oa = jnp.concatenate([oa_ref[p] for p in range(N_PAIRS)], axis=1)
    merged = None
    for g, o_branch in enumerate((oa, ob_ref[...])):
        u = (o_branch * act_ref[:, g * bw:(g + 1) * bw]).astype(jnp.bfloat16)
        y = jnp.dot(u, wb_ref[g], preferred_element_type=jnp.float32)
        term = act_ref[:, 2 * bw + g * D_MODEL:2 * bw + (g + 1) * D_MODEL] * y
        merged = term if merged is None else merged + term
    o_ref[0] = x_ref[0] + jnp.dot(merged.astype(jnp.bfloat16), wo_ref[...],
                                  preferred_element_type=jnp.float32)


def _mix(qkv, x, rel_bias, sinks, norm_w, w_in, w_branch, w_out):
    b, seq, d = x.shape
    assert all(len(s) == N_HEADS // 2 for s in _SW_STACKS)
    bucket = jnp.asarray(_t5_bucket_table())
    smem = pl.BlockSpec(memory_space=pltpu.SMEM)
    sub = TILE // SW_BLOCK
    lead_tiles = LEAD // TILE
    lp = seq + LEAD
    once = dict(pipeline_mode=pl.Buffered(1))
    const2 = lambda bi, i: (0, 0)
    lead_blocks = lambda blk: pl.BlockSpec((1, N_PAIRS, TILE, LANES),
                                           lambda bi, i: (bi, blk // N_PAIRS, lead_tiles - 1, 0))
    return pl.pallas_call(
        _mix_kernel,
        grid=(b, seq // TILE),
        in_specs=[
            smem, smem,
            pl.BlockSpec((SW_BLOCK, SW_BLOCK), const2, **once),
            pl.BlockSpec((1, QKV_BLOCKS, TILE, LANES), lambda bi, i: (bi, 0, i + lead_tiles, 0)),
            lead_blocks(BLK_KA), lead_blocks(BLK_VA),
            pl.BlockSpec((1, 4, SW_BLOCK, LANES),
                         lambda bi, i: (bi, BLK_KVB // 4, (i + lead_tiles) * sub - 1, 0)),
            pl.BlockSpec((1, TILE, d), lambda bi, i: (bi, i, 0)),
            pl.BlockSpec((1, d), const2, **once),
            *[_w_piece_spec(d, *piece) for piece in _GATE_PIECES],
            pl.BlockSpec((2, BRANCH_WIDTH, d), lambda bi, i: (0, 0, 0), **once),
            pl.BlockSpec((d, d), const2, **once),
        ],
        out_specs=pl.BlockSpec((1, TILE, d), lambda bi, i: (bi, i, 0)),
        out_shape=jax.ShapeDtypeStruct((b, seq, d), jnp.float32),
        scratch_shapes=[pltpu.VMEM((N_HEADS, SW_BLOCK, SW_BLOCK), jnp.float32),
                        pltpu.VMEM((TILE, d), jnp.bfloat16),
                        pltpu.VMEM((TILE, GATE_WIDTH), jnp.float32),
                        pltpu.VMEM((TILE, BRANCH_WIDTH), jnp.float32),
                        pltpu.VMEM((1, N_PAIRS, lp, LANES), jnp.bfloat16),
                        pltpu.VMEM((1, N_PAIRS, lp, LANES), jnp.bfloat16),
                        ] + _SB_SCRATCH,
        compiler_params=pltpu.CompilerParams(
            dimension_semantics=("arbitrary", "arbitrary"), vmem_limit_bytes=VMEM_LIMIT),
        name="attend_mix",
    )(rel_bias, sinks, bucket, *([qkv] * 4), x, norm_w,
      *([w_in] * len(_GATE_PIECES)), w_branch, w_out)


def kernel(x, meta, rel_bias, norm_w, w_in, q_gain, k_gain, sinks, w_branch, w_out):
    b, seq, d = x.shape
    assert d == D_MODEL and norm_w.shape[0] == 1, "single-layer block of width 1024 only"
    assert meta.shape == (N_META, d) and seq % ROW_TILE == 0
    assert w_in.shape == (1, d, 4 * BRANCH_WIDTH + 2 * SW_KV_WIDTH + GATE_WIDTH)
    gq = jnp.tile(q_gain[0], 4)[None, :] * (HEAD_DIM ** -0.5)
    gk = jnp.tile(k_gain[0], 4)[None, :]
    nw = norm_w[0][None, :]

    qkv = _qkv_project(x, meta.astype(x.dtype), nw, w_in, gq, gk)
    return _mix(qkv, x, rel_bias, sinks[0], nw, w_in, w_branch[0].astype(jnp.bfloat16),
                w_out[0].astype(jnp.bfloat16))
```

```python
import math

import jax
import jax.numpy as jnp
import numpy as np
from jax import lax
from jax.experimental import pallas as pl
from jax.experimental.pallas import tpu as pltpu

D_MODEL = 1024
HEAD_DIM = 64
BRANCH_WIDTH = 512
SW_KV_WIDTH = 128
N_HEADS = BRANCH_WIDTH // HEAD_DIM
N_META = 16
WINDOW = 128
SW_BLOCK = 128
N_BUCKETS = 32
MAX_DISTANCE = 128
RMS_EPS = 1e-6

LANES = 128
TILE = 256
LEAD = 2 * TILE
FIRST_VALID = LEAD - N_META
ROW_TILE = 512
N_PAIRS = BRANCH_WIDTH // LANES
QKV_PROJ_WIDTH = 4 * BRANCH_WIDTH + 2 * SW_KV_WIDTH
QKV_BLOCKS = 4 * N_PAIRS + 4
BLK_QA, BLK_KA, BLK_VA, BLK_QB, BLK_KVB = 0, N_PAIRS, 2 * N_PAIRS, 3 * N_PAIRS, 4 * N_PAIRS
GATE_WIDTH = 2 * BRANCH_WIDTH + 2 * D_MODEL
W_CHUNK = 2 * LANES
_QKV_SRC_CHUNKS = (0, 1, 2, 3, 4, 5, 8, 9, 10)
_QKV_PIECES = ((0, 6), (8, 4))
_GATE_SRC_CHUNKS = (6, 7) + tuple(range(11, 21))
_GATE_PIECES = ((6, 2), (11, 1), (12, 4), (16, 4), (20, 1))
EXP_ZERO_BELOW = -105.0
VMEM_LIMIT = 56 * 1024 * 1024

_NT = (((1,), (1,)), ((), ()))


def _rms_rows(x, w):
    ms = jnp.mean(x * x, axis=-1, keepdims=True)
    return x * lax.rsqrt(ms + RMS_EPS) * w


def _exp_neg_abs(x):
    return jnp.exp2(jnp.abs(x) * (-math.log2(math.e)))


def _head_lanes():
    lane = lax.broadcasted_iota(jnp.int32, (1, LANES), 1)
    return lane < HEAD_DIM, lane >= HEAD_DIM


def _w_chunk(pieces, c):
    for ref, first, count in pieces:
        if first <= c < first + count:
            off = (c - first) * W_CHUNK
            return ref[0, :, off:off + W_CHUNK].astype(jnp.bfloat16)
    raise ValueError(c)


def _w_piece_spec(d, first, count):
    assert first % count == 0
    return pl.BlockSpec((1, d, count * W_CHUNK), lambda bi, i: (0, 0, first // count),
                        pipeline_mode=pl.Buffered(1))


def _qkv_kernel(x_ref, meta_ref, nw_ref, wa_ref, wb_ref, gq_ref, gk_ref, wbr_ref, wo_ref,
                o_ref, wbr_bf_ref, wo_bf_ref):
    i = pl.program_id(1)
    chunk = W_CHUNK
    wbr_bf_ref[...] = wbr_ref[...].astype(wbr_bf_ref.dtype)
    wo_bf_ref[...] = wo_ref[...].astype(wo_bf_ref.dtype)
    pieces = [(wa_ref,) + _QKV_PIECES[0], (wb_ref,) + _QKV_PIECES[1]]
    r = lax.broadcasted_iota(jnp.int32, (chunk, chunk), 0) // HEAD_DIM
    c = lax.broadcasted_iota(jnp.int32, (chunk, chunk), 1) // HEAD_DIM
    head_ones = jnp.where(r == c, 1.0, 0.0).astype(jnp.bfloat16)

    def head_norm(a, gain):
        ssq = jnp.dot((a * a).astype(jnp.bfloat16), head_ones, preferred_element_type=jnp.float32)
        return a * lax.rsqrt(ssq * (1.0 / HEAD_DIM) + RMS_EPS) * gain

    def project(rows, r0, nrows):
        def emit(blk, a):
            o_ref[0, blk, r0:r0 + nrows, :] = a.astype(o_ref.dtype)

        xn = _rms_rows(rows, nw_ref[...]).astype(jnp.bfloat16)
        n_chunks = QKV_PROJ_WIDTH // chunk
        first_norm = BLK_QB // 2
        proj = lambda j: jnp.dot(xn, _w_chunk(pieces, _QKV_SRC_CHUNKS[j]),
                                 preferred_element_type=jnp.float32)
        to_norm = {j: proj(j) for j in range(first_norm, n_chunks)}
        for j in range(first_norm):
            a = proj(j)
            if 2 * j < BLK_KA:
                a = a * (HEAD_DIM ** -0.5)
            emit(2 * j, a[:, :LANES])
            emit(2 * j + 1, a[:, LANES:])
        for j, a in to_norm.items():
            if 2 * j < BLK_KVB:
                a = head_norm(a, gq_ref[...])
                emit(2 * j, a[:, :LANES])
                emit(2 * j + 1, a[:, LANES:])
            else:
                kb = head_norm(a, gk_ref[...])[:, :LANES]
                vb = a[:, LANES:]
                emit(BLK_KVB, kb)
                emit(BLK_KVB + 1, pltpu.roll(kb, HEAD_DIM, 1))
                emit(BLK_KVB + 2, vb)
                emit(BLK_KVB + 3, pltpu.roll(vb, HEAD_DIM, 1))

    @pl.when(i == 0)
    def _():
        o_ref[0, :, :FIRST_VALID, :] = jnp.zeros((QKV_BLOCKS, FIRST_VALID, LANES), o_ref.dtype)
        project(meta_ref[...], FIRST_VALID, N_META)

    @pl.when(i > 0)
    def _():
        project(x_ref[0], 0, ROW_TILE)


def _qkv_project(x, meta, norm_w, w_in, gq, gk, w_branch, w_out):
    b, seq, d = x.shape
    assert LEAD == ROW_TILE and seq % ROW_TILE == 0
    n_tiles = (seq + LEAD) // ROW_TILE
    const = lambda bi, i: (0, 0)
    slab = d // (n_tiles - 1)
    assert slab * (n_tiles - 1) == d and w_branch.shape == (d, d) and w_out.shape == (d, d)
    slab_spec = pl.BlockSpec(
        (slab, d), lambda bi, i: (jnp.where(bi == 0, jnp.maximum(i - 1, 0), n_tiles - 2), 0))
    return pl.pallas_call(
        _qkv_kernel,
        grid=(b, n_tiles),
        in_specs=[
            pl.BlockSpec((1, ROW_TILE, d), lambda bi, i: (bi, jnp.maximum(i - 1, 0), 0)),
            pl.BlockSpec((N_META, d), const),
            pl.BlockSpec((1, d), const),
            _w_piece_spec(d, *_QKV_PIECES[0]),
            _w_piece_spec(d, *_QKV_PIECES[1]),
            pl.BlockSpec((1, 2 * LANES), const),
            pl.BlockSpec((1, 2 * LANES), const),
            slab_spec, slab_spec,
        ],
        out_specs=[pl.BlockSpec((1, QKV_BLOCKS, ROW_TILE, LANES), lambda bi, i: (bi, 0, i, 0)),
                   slab_spec, slab_spec],
        out_shape=[jax.ShapeDtypeStruct((b, QKV_BLOCKS, seq + LEAD, LANES), jnp.bfloat16),
                   jax.ShapeDtypeStruct((d, d), jnp.bfloat16),
                   jax.ShapeDtypeStruct((d, d), jnp.bfloat16)],
        compiler_params=pltpu.CompilerParams(
            dimension_semantics=("arbitrary", "arbitrary"), vmem_limit_bytes=VMEM_LIMIT),
        name="qkv_project",
    )(x, meta, norm_w, w_in, w_in, gq, gk, w_branch, w_out)


def _sb_tile(q_ref, k_ref, v_ref, acc_ref, carry_ref, negl_ref, logsig_ref, first_ref, w_ref,
             fillers=()):
    pending = list(fillers)
    qt = pl.program_id(1) + LEAD // TILE
    head_lanes = _head_lanes()
    row = lax.broadcasted_iota(jnp.int32, (TILE, TILE), 0)
    col = lax.broadcasted_iota(jnp.int32, (TILE, TILE), 1)
    later_sum = jnp.where(row > col, 1.0, 0.0).astype(jnp.bfloat16)
    masks = {"diag": row > col, "lead": col >= FIRST_VALID - TILE, "full": None,
             "prev": col >= FIRST_VALID - (qt - 1) * TILE}

    def all_pairs(j, mode, fill=False):
        mask = masks[mode]
        start = pl.multiple_of(j * TILE, TILE)
        for p in range(N_PAIRS):
            q2 = q_ref[0, p]
            k2 = k_ref[0, p, pl.ds(start, TILE), :]
            for h in range(2):
                qm = jnp.where(head_lanes[h], q2, jnp.zeros_like(q2))
                z = lax.dot_general(qm, k2, _NT, preferred_element_type=jnp.float32)
                neg_l = jnp.maximum(z, 0.0) + jnp.log(1.0 + _exp_neg_abs(z))
                if mask is not None:
                    neg_l = jnp.where(mask, neg_l, 0.0)
                negl_ref[p, h] = neg_l.astype(jnp.bfloat16)
                logsig_ref[p, h] = z - neg_l
                first_ref[p, h] = neg_l[:, :LANES]
                if fill and pending:
                    pending.pop(0)()
        for p in range(N_PAIRS):
            for h in range(2):
                later = jnp.dot(negl_ref[p, h], later_sum, preferred_element_type=jnp.float32)
                total = jnp.broadcast_to(later[:, :1] + first_ref[p, h][:, :1], (TILE, LANES))
                logw = logsig_ref[p, h] - later
                if mode != "diag":
                    carry = carry_ref[p, h]
                    logw = logw - jnp.concatenate([carry] * (TILE // LANES), axis=1)
                    carry_ref[p, h] = carry + total
                else:
                    carry_ref[p, h] = total
                w = jnp.exp(logw)
                if mask is not None:
                    w = jnp.where(mask, w, 0.0)
                w_ref[p, h] = w.astype(jnp.bfloat16)
        for p in range(N_PAIRS):
            v2 = v_ref[0, p, pl.ds(start, TILE), :]
            upd = None
            for h in range(2):
                vm = jnp.where(head_lanes[h], v2, jnp.zeros_like(v2))
                pv = jnp.dot(w_ref[p, h], vm, preferred_element_type=jnp.float32)
                upd = pv if upd is None else upd + pv
            if mode == "diag":
                acc_ref[p] = upd
            else:
                acc_ref[p] += upd

    def still_visible():
        return jnp.min(carry_ref[...]) < -EXP_ZERO_BELOW

    first_full = LEAD // TILE
    all_pairs(qt, "diag", fill=True)
    all_pairs(qt - 1, "prev", fill=True)
    while pending:
        pending.pop(0)()

    def cond(state):
        j, go = state
        return (j >= first_full) & go

    def body(state):
        j, _ = state
        all_pairs(j, "full")
        return j - 1, still_visible()

    j_end, go = lax.while_loop(cond, body, (qt - 2, still_visible()))

    @pl.when(go & (qt > first_full))
    def _():
        all_pairs(first_full - 1, "lead")


_SB_SCRATCH = [pltpu.VMEM((N_PAIRS, TILE, LANES), jnp.float32),
               pltpu.VMEM((N_PAIRS, 2, TILE, LANES), jnp.float32),
               pltpu.VMEM((N_PAIRS, 2, TILE, TILE), jnp.bfloat16),
               pltpu.VMEM((N_PAIRS, 2, TILE, TILE), jnp.float32),
               pltpu.VMEM((N_PAIRS, 2, TILE, LANES), jnp.float32),
               pltpu.VMEM((N_PAIRS, 2, TILE, TILE), jnp.bfloat16)]


def _t5_bucket_table():
    r = np.arange(SW_BLOCK)[:, None]
    l = np.arange(SW_BLOCK)[None, :]
    n = np.where(l > r, SW_BLOCK + r - l, r - l)
    assert WINDOW == SW_BLOCK and n.min() >= 0 and n.max() < WINDOW
    max_exact = N_BUCKETS // 2
    large = max_exact + (np.log(np.maximum(n, 1) / max_exact)
                         / math.log(MAX_DISTANCE / max_exact)
                         * (N_BUCKETS - max_exact)).astype(np.int32)
    large = np.minimum(large, N_BUCKETS - 1)
    return np.where(n < max_exact, n, large).astype(np.int32)


_SW_GROUP = N_HEADS // (SW_KV_WIDTH // HEAD_DIM)
_SW_STACKS = tuple(
    tuple(h for par in range(2) for h in range(N_HEADS)
          if h % 2 == par and (0 if (h // _SW_GROUP) == par else 1) == variant)
    for variant in range(2))
_SW_STACK_ROWS = len(_SW_STACKS[0]) * SW_BLOCK


def _swa_bias_init(relb_ref, bucket_ref, bias_ref):
    @pl.when((pl.program_id(0) == 0) & (pl.program_id(1) == 0))
    def _():
        bucket = bucket_ref[...]
        for h in range(N_HEADS):
            t = jnp.zeros(bucket.shape, jnp.float32)
            for bkt in range(N_BUCKETS):
                t = jnp.where(bucket == bkt, relb_ref[bkt, h], t)
            bias_ref[h] = t


def _swa_block(sb, sink_ref, q_ref, kvp_ref, kvc_ref, bias_ref, ob_ref):
    i = pl.program_id(1)
    head_lanes = _head_lanes()
    lane = lax.broadcasted_iota(jnp.int32, (SW_BLOCK, SW_BLOCK), 1)
    from_prev = lane > lax.broadcasted_iota(jnp.int32, (SW_BLOCK, SW_BLOCK), 0)
    lane1 = lax.broadcasted_iota(jnp.int32, (1, SW_BLOCK), 1)
    lead_pen = jnp.where((i == 0) & (lane1 < FIRST_VALID - (LEAD - SW_BLOCK)), -jnp.inf, 0.0)
    ones_cols = [jnp.broadcast_to(jnp.where(m, 1.0, 0.0).astype(jnp.bfloat16), (2 * SW_BLOCK, LANES))
                 for m in head_lanes]
    n_half = len(_SW_STACKS[0]) // 2
    rows = slice(sb * SW_BLOCK, (sb + 1) * SW_BLOCK)

    def kv_rows(blk):
        cur = kvc_ref[0, blk, :(sb + 1) * SW_BLOCK, :]
        if sb == 0:
            return jnp.concatenate([kvp_ref[0, blk], cur], axis=0)
        return cur[(sb - 1) * SW_BLOCK:]

    num = [None] * N_PAIRS
    den = [None] * N_PAIRS
    sink_term = [None] * N_HEADS
    for variant, heads in enumerate(_SW_STACKS):
        k2 = kv_rows(variant)
        v2 = kv_rows(2 + variant)
        qs = []
        for h in heads:
            q2 = q_ref[0, h // 2, rows, :]
            qs.append(jnp.where(head_lanes[h % 2], q2, jnp.zeros_like(q2)))
        logits = lax.dot_general(jnp.concatenate(qs, axis=0), k2, _NT,
                                 preferred_element_type=jnp.float32)
        weights = []
        for slot, h in enumerate(heads):
            lg = logits[slot * SW_BLOCK:(slot + 1) * SW_BLOCK]
            prev = lg[:, :SW_BLOCK]
            if sb == 0:
                prev = prev + lead_pen
            folded = jnp.where(from_prev, prev, lg[:, SW_BLOCK:]) + bias_ref[h]
            m = jnp.maximum(jnp.max(folded, axis=-1, keepdims=True), sink_ref[h])
            m = jnp.broadcast_to(m, folded.shape)
            e = jnp.exp(folded - m)
            sink_term[h] = jnp.exp(sink_ref[h] - m)
            zero = jnp.zeros_like(e)
            weights.append(jnp.concatenate(
                [jnp.where(from_prev, e, zero), jnp.where(from_prev, zero, e)],
                axis=1).astype(jnp.bfloat16))
        for par in range(2):
            vm = jnp.where(head_lanes[par], v2, jnp.zeros_like(v2))
            res = jnp.dot(jnp.concatenate(weights[par * n_half:(par + 1) * n_half], axis=0),
                          jnp.concatenate([vm, ones_cols[par]], axis=1),
                          preferred_element_type=jnp.float32)
            for s in range(n_half):
                p = heads[par * n_half + s] // 2
                part = res[s * SW_BLOCK:(s + 1) * SW_BLOCK]
                num[p] = part[:, :LANES] if num[p] is None else num[p] + part[:, :LANES]
                den[p] = part[:, LANES:] if den[p] is None else den[p] + part[:, LANES:]
    for p in range(N_PAIRS):
        denom = den[p] + jnp.where(head_lanes[0], sink_term[2 * p], sink_term[2 * p + 1])
        ob_ref[rows, p * LANES:(p + 1) * LANES] = num[p] / denom


def _mix_kernel(relb_ref, sink_ref, bucket_ref, tile_ref, ka_lead_ref, va_lead_ref, kvp_ref,
                x_ref, nw_ref, wg0_ref, wg1_ref, wg2_ref, wg3_ref, wg4_ref, wb_ref, wo_ref, o_ref,
                bias_ref, xn_ref, act_ref, ob_ref, ka_ref, va_ref, oa_ref, *sb_scratch):
    i = pl.program_id(1)
    lead_tile = LEAD // TILE - 1
    qa_ref, ka_tile_ref, va_tile_ref, qb_ref = (
        tile_ref.at[:, blk:blk + N_PAIRS] for blk in (BLK_QA, BLK_KA, BLK_VA, BLK_QB))
    kvc_ref = tile_ref.at[:, BLK_KVB:BLK_KVB + 4]

    @pl.when(i == 0)
    def _():
        ka_ref[0, :, lead_tile * TILE:LEAD, :] = ka_lead_ref[0]
        va_ref[0, :, lead_tile * TILE:LEAD, :] = va_lead_ref[0]

    own_rows = pl.ds(pl.multiple_of((i + LEAD // TILE) * TILE, TILE), TILE)
    ka_ref[0, :, own_rows, :] = ka_tile_ref[0]
    va_ref[0, :, own_rows, :] = va_tile_ref[0]
    _swa_bias_init(relb_ref, bucket_ref, bias_ref)
    xn_ref[...] = _rms_rows(x_ref[0], nw_ref[...]).astype(jnp.bfloat16)
    chunk = W_CHUNK
    bw = BRANCH_WIDTH
    pieces = [(r,) + p for r, p in zip((wg0_ref, wg1_ref, wg2_ref, wg3_ref, wg4_ref), _GATE_PIECES)]

    def gate_chunk(c):
        def run():
            cols = slice(c * chunk, (c + 1) * chunk)
            a = jnp.dot(xn_ref[...], _w_chunk(pieces, _GATE_SRC_CHUNKS[c]),
                        preferred_element_type=jnp.float32)
            s = jax.nn.sigmoid(a)
            act_ref[:, cols] = a * s if c * chunk < 2 * bw else s
        return run

    _sb_tile(qa_ref, ka_ref, va_ref, oa_ref, *sb_scratch,
             fillers=[gate_chunk(c) for c in range(GATE_WIDTH // chunk)])
    for sb in range(TILE // SW_BLOCK):
        _swa_block(sb, sink_ref, qb_ref, kvp_ref, kvc_ref, bias_ref, ob_ref)
    oa = jnp.concatenate([oa_ref[p] for p in range(N_PAIRS)], axis=1)
    merged = None
    for g, o_branch in enumerate((oa, ob_ref[...])):
        u = (o_branch * act_ref[:, g * bw:(g + 1) * bw]).astype(jnp.bfloat16)
        y = jnp.dot(u, wb_ref[g], preferred_element_type=jnp.float32)
        term = act_ref[:, 2 * bw + g * D_MODEL:2 * bw + (g + 1) * D_MODEL] * y
        merged = term if merged is None else merged + term
    o_ref[0] = x_ref[0] + jnp.dot(merged.astype(jnp.bfloat16), wo_ref[...],
                                  preferred_element_type=jnp.float32)


def _mix(qkv, x, rel_bias, sinks, norm_w, w_in, w_branch, w_out):
    b, seq, d = x.shape
    assert all(len(s) == N_HEADS // 2 for s in _SW_STACKS)
    bucket = jnp.asarray(_t5_bucket_table())
    smem = pl.BlockSpec(memory_space=pltpu.SMEM)
    sub = TILE // SW_BLOCK
    lead_tiles = LEAD // TILE
    lp = seq + LEAD
    once = dict(pipeline_mode=pl.Buffered(1))
    const2 = lambda bi, i: (0, 0)
    lead_blocks = lambda blk: pl.BlockSpec((1, N_PAIRS, TILE, LANES),
                                           lambda bi, i: (bi, blk // N_PAIRS, lead_tiles - 1, 0))
    return pl.pallas_call(
        _mix_kernel,
        grid=(b, seq // TILE),
        in_specs=[
            smem, smem,
            pl.BlockSpec((SW_BLOCK, SW_BLOCK), const2, **once),
            pl.BlockSpec((1, QKV_BLOCKS, TILE, LANES), lambda bi, i: (bi, 0, i + lead_tiles, 0)),
            lead_blocks(BLK_KA), lead_blocks(BLK_VA),
            pl.BlockSpec((1, 4, SW_BLOCK, LANES),
                         lambda bi, i: (bi, BLK_KVB // 4, (i + lead_tiles) * sub - 1, 0)),
            pl.BlockSpec((1, TILE, d), lambda bi, i: (bi, i, 0)),
            pl.BlockSpec((1, d), const2, **once),
            *[_w_piece_spec(d, *piece) for piece in _GATE_PIECES],
            pl.BlockSpec((2, BRANCH_WIDTH, d), lambda bi, i: (0, 0, 0), **once),
            pl.BlockSpec((d, d), const2, **once),
        ],
        out_specs=pl.BlockSpec((1, TILE, d), lambda bi, i: (bi, i, 0)),
        out_shape=jax.ShapeDtypeStruct((b, seq, d), jnp.float32),
        scratch_shapes=[pltpu.VMEM((N_HEADS, SW_BLOCK, SW_BLOCK), jnp.float32),
                        pltpu.VMEM((TILE, d), jnp.bfloat16),
                        pltpu.VMEM((TILE, GATE_WIDTH), jnp.float32),
                        pltpu.VMEM((TILE, BRANCH_WIDTH), jnp.float32),
                        pltpu.VMEM((1, N_PAIRS, lp, LANES), jnp.bfloat16),
                        pltpu.VMEM((1, N_PAIRS, lp, LANES), jnp.bfloat16),
                        ] + _SB_SCRATCH,
        compiler_params=pltpu.CompilerParams(
            dimension_semantics=("arbitrary", "arbitrary"), vmem_limit_bytes=VMEM_LIMIT),
        name="attend_mix",
    )(rel_bias, sinks, bucket, *([qkv] * 4), x, norm_w,
      *([w_in] * len(_GATE_PIECES)), w_branch, w_out)


def kernel(x, meta, rel_bias, norm_w, w_in, q_gain, k_gain, sinks, w_branch, w_out):
    b, seq, d = x.shape
    assert d == D_MODEL and norm_w.shape[0] == 1, "single-layer block of width 1024 only"
    assert meta.shape == (N_META, d) and seq % ROW_TILE == 0
    assert w_in.shape == (1, d, 4 * BRANCH_WIDTH + 2 * SW_KV_WIDTH + GATE_WIDTH)
    gq = jnp.tile(q_gain[0], 4)[None, :] * (HEAD_DIM ** -0.5)
    gk = jnp.tile(k_gain[0], 4)[None, :]
    nw = norm_w[0][None, :]

    assert w_branch.shape == (1, 2, BRANCH_WIDTH, d) and w_out.shape == (1, d, d)
    qkv, w_branch_bf, w_out_bf = _qkv_project(x, meta.astype(x.dtype), nw, w_in, gq, gk,
                                              w_branch[0].reshape(d, d), w_out[0])
    return _mix(qkv, x, rel_bias, sinks[0], nw, w_in, w_branch_bf.reshape(2, BRANCH_WIDTH, d),
                w_out_bf)
```

```python
import math

import jax
import jax.numpy as jnp
import numpy as np
from jax import lax
from jax.experimental import pallas as pl
from jax.experimental.pallas import tpu as pltpu

D_MODEL = 1024
HEAD_DIM = 64
BRANCH_WIDTH = 512
SW_KV_WIDTH = 128
N_HEADS = BRANCH_WIDTH // HEAD_DIM
N_META = 16
WINDOW = 128
SW_BLOCK = 128
N_BUCKETS = 32
MAX_DISTANCE = 128
RMS_EPS = 1e-6

LANES = 128
TILE = 256
LEAD = 2 * TILE
FIRST_VALID = LEAD - N_META
ROW_TILE = 512
N_PAIRS = BRANCH_WIDTH // LANES
QKV_PROJ_WIDTH = 4 * BRANCH_WIDTH + 2 * SW_KV_WIDTH
QKV_BLOCKS = 4 * N_PAIRS + 4
BLK_QA, BLK_KA, BLK_VA, BLK_QB, BLK_KVB = 0, N_PAIRS, 2 * N_PAIRS, 3 * N_PAIRS, 4 * N_PAIRS
GATE_WIDTH = 2 * BRANCH_WIDTH + 2 * D_MODEL
W_CHUNK = 2 * LANES
_QKV_SRC_CHUNKS = (0, 1, 2, 3, 4, 5, 8, 9, 10)
_QKV_PIECES = ((0, 6), (8, 4))
_GATE_COLS = ((6 * W_CHUNK, 8 * W_CHUNK), (11 * W_CHUNK, 21 * W_CHUNK))
EXP_ZERO_BELOW = -105.0
VMEM_LIMIT = 56 * 1024 * 1024

_NT = (((1,), (1,)), ((), ()))


def _rms_rows(x, w):
    ms = jnp.mean(x * x, axis=-1, keepdims=True)
    return x * lax.rsqrt(ms + RMS_EPS) * w


def _exp_neg_abs(x):
    return jnp.exp2(jnp.abs(x) * (-math.log2(math.e)))


def _head_lanes():
    lane = lax.broadcasted_iota(jnp.int32, (1, LANES), 1)
    return lane < HEAD_DIM, lane >= HEAD_DIM


def _w_chunk(pieces, c):
    for ref, first, count in pieces:
        if first <= c < first + count:
            off = (c - first) * W_CHUNK
            return ref[0, :, off:off + W_CHUNK].astype(jnp.bfloat16)
    raise ValueError(c)


def _w_piece_spec(d, first, count):
    assert first % count == 0
    return pl.BlockSpec((1, d, count * W_CHUNK), lambda bi, i: (0, 0, first // count),
                        pipeline_mode=pl.Buffered(1))


def _qkv_kernel(x_ref, meta_ref, nw_ref, wa_ref, wb_ref, gq_ref, gk_ref, win_ref, wbr_ref, wo_ref,
                o_ref, wg_bf_ref, wbr_bf_ref, wo_bf_ref):
    i = pl.program_id(1)
    chunk = W_CHUNK
    def cast_weight_slabs():
        wg_bf_ref[...] = jnp.concatenate([win_ref[0, :, lo:hi] for lo, hi in _GATE_COLS],
                                         axis=1).astype(wg_bf_ref.dtype)
        wbr_bf_ref[...] = wbr_ref[...].astype(wbr_bf_ref.dtype)
        wo_bf_ref[...] = wo_ref[...].astype(wo_bf_ref.dtype)

    pieces =[(wa_ref,) + _QKV_PIECES[0], (wb_ref,) + _QKV_PIECES[1]]
    r = lax.broadcasted_iota(jnp.int32, (chunk, chunk), 0) // HEAD_DIM
    c = lax.broadcasted_iota(jnp.int32, (chunk, chunk), 1) // HEAD_DIM
    head_ones = jnp.where(r == c, 1.0, 0.0).astype(jnp.bfloat16)

    def head_norm(a, gain):
        ssq = jnp.dot((a * a).astype(jnp.bfloat16), head_ones, preferred_element_type=jnp.float32)
        return a * lax.rsqrt(ssq * (1.0 / HEAD_DIM) + RMS_EPS) * gain

    def project(rows, r0, nrows):
        def emit(blk, a):
            o_ref[0, blk, r0:r0 + nrows, :] = a.astype(o_ref.dtype)

        xn = _rms_rows(rows, nw_ref[...]).astype(jnp.bfloat16)
        n_chunks = QKV_PROJ_WIDTH // chunk
        first_norm = BLK_QB // 2
        proj = lambda j: jnp.dot(xn, _w_chunk(pieces, _QKV_SRC_CHUNKS[j]),
                                 preferred_element_type=jnp.float32)
        to_norm = {j: proj(j) for j in range(first_norm, n_chunks)}
        for j in range(first_norm):
            a = proj(j)
            if 2 * j < BLK_KA:
                a = a * (HEAD_DIM ** -0.5)
            emit(2 * j, a[:, :LANES])
            emit(2 * j + 1, a[:, LANES:])
        for j, a in to_norm.items():
            if 2 * j < BLK_KVB:
                a = head_norm(a, gq_ref[...])
                emit(2 * j, a[:, :LANES])
                emit(2 * j + 1, a[:, LANES:])
            else:
                kb = head_norm(a, gk_ref[...])[:, :LANES]
                vb = a[:, LANES:]
                emit(BLK_KVB, kb)
                emit(BLK_KVB + 1, pltpu.roll(kb, HEAD_DIM, 1))
                emit(BLK_KVB + 2, vb)
                emit(BLK_KVB + 3, pltpu.roll(vb, HEAD_DIM, 1))

    @pl.when(i == 0)
    def _():
        o_ref[0, :, :FIRST_VALID, :] = jnp.zeros((QKV_BLOCKS, FIRST_VALID, LANES), o_ref.dtype)
        project(meta_ref[...], FIRST_VALID, N_META)
        cast_weight_slabs()

    @pl.when(i > 0)
    def _():
        project(x_ref[0], 0, ROW_TILE)
        cast_weight_slabs()


def _qkv_project(x, meta, norm_w, w_in, gq, gk, w_branch, w_out):
    b, seq, d = x.shape
    assert LEAD == ROW_TILE and seq % ROW_TILE == 0
    n_tiles = (seq + LEAD) // ROW_TILE
    const = lambda bi, i: (0, 0)
    slab = d // (n_tiles - 1)
    assert slab * (n_tiles - 1) == d and w_branch.shape == (d, d) and w_out.shape == (d, d)
    slab_of = lambda bi, i: jnp.where(bi == 0, jnp.maximum(i - 1, 0), n_tiles - 2)
    slab_spec = pl.BlockSpec((slab, d), lambda bi, i: (slab_of(bi, i), 0))
    gate_slab_spec = pl.BlockSpec((slab, GATE_WIDTH), lambda bi, i: (slab_of(bi, i), 0))
    w_in_slab_spec = pl.BlockSpec((1, slab, w_in.shape[2]), lambda bi, i: (0, slab_of(bi, i), 0))
    return pl.pallas_call(
        _qkv_kernel,
        grid=(b, n_tiles),
        in_specs=[
            pl.BlockSpec((1, ROW_TILE, d), lambda bi, i: (bi, jnp.maximum(i - 1, 0), 0)),
            pl.BlockSpec((N_META, d), const),
            pl.BlockSpec((1, d), const),
            _w_piece_spec(d, *_QKV_PIECES[0]),
            _w_piece_spec(d, *_QKV_PIECES[1]),
            pl.BlockSpec((1, 2 * LANES), const),
            pl.BlockSpec((1, 2 * LANES), const),
            w_in_slab_spec, slab_spec, slab_spec,
        ],
        out_specs=[pl.BlockSpec((1, QKV_BLOCKS, ROW_TILE, LANES), lambda bi, i: (bi, 0, i, 0)),
                   gate_slab_spec, slab_spec, slab_spec],
        out_shape=[jax.ShapeDtypeStruct((b, QKV_BLOCKS, seq + LEAD, LANES), jnp.bfloat16),
                   jax.ShapeDtypeStruct((d, GATE_WIDTH), jnp.bfloat16),
                   jax.ShapeDtypeStruct((d, d), jnp.bfloat16),
                   jax.ShapeDtypeStruct((d, d), jnp.bfloat16)],
        compiler_params=pltpu.CompilerParams(
            dimension_semantics=("arbitrary", "arbitrary"), vmem_limit_bytes=VMEM_LIMIT),
        name="qkv_project",
    )(x, meta, norm_w, w_in, w_in, gq, gk, w_in, w_branch, w_out)


def _sb_tile(q_ref, k_ref, v_ref, acc_ref, carry_ref, negl_ref, logsig_ref, first_ref, w_ref,
             fillers=()):
    pending = list(fillers)
    qt = pl.program_id(1) + LEAD // TILE
    head_lanes = _head_lanes()
    row = lax.broadcasted_iota(jnp.int32, (TILE, TILE), 0)
    col = lax.broadcasted_iota(jnp.int32, (TILE, TILE), 1)
    later_sum = jnp.where(row > col, 1.0, 0.0).astype(jnp.bfloat16)
    masks = {"diag": row > col, "lead": col >= FIRST_VALID - TILE, "full": None,
             "prev": col >= FIRST_VALID - (qt - 1) * TILE}

    def all_pairs(j, mode, fill=False):
        mask = masks[mode]
        start = pl.multiple_of(j * TILE, TILE)
        for p in range(N_PAIRS):
            q2 = q_ref[0, p]
            k2 = k_ref[0, p, pl.ds(start, TILE), :]
            for h in range(2):
                qm = jnp.where(head_lanes[h], q2, jnp.zeros_like(q2))
                z = lax.dot_general(qm, k2, _NT, preferred_element_type=jnp.float32)
                neg_l = jnp.maximum(z, 0.0) + jnp.log(1.0 + _exp_neg_abs(z))
                if mask is not None:
                    neg_l = jnp.where(mask, neg_l, 0.0)
                negl_ref[p, h] = neg_l.astype(jnp.bfloat16)
                logsig_ref[p, h] = z - neg_l
                first_ref[p, h] = neg_l[:, :LANES]
                if fill and pending:
                    pending.pop(0)()
        for p in range(N_PAIRS):
            for h in range(2):
                later = jnp.dot(negl_ref[p, h], later_sum, preferred_element_type=jnp.float32)
                total = jnp.broadcast_to(later[:, :1] + first_ref[p, h][:, :1], (TILE, LANES))
                logw = logsig_ref[p, h] - later
                if mode != "diag":
                    carry = carry_ref[p, h]
                    logw = logw - jnp.concatenate([carry] * (TILE // LANES), axis=1)
                    carry_ref[p, h] = carry + total
                else:
                    carry_ref[p, h] = total
                w = jnp.exp(logw)
                if mask is not None:
                    w = jnp.where(mask, w, 0.0)
                w_ref[p, h] = w.astype(jnp.bfloat16)
        for p in range(N_PAIRS):
            v2 = v_ref[0, p, pl.ds(start, TILE), :]
            upd = None
            for h in range(2):
                vm = jnp.where(head_lanes[h], v2, jnp.zeros_like(v2))
                pv = jnp.dot(w_ref[p, h], vm, preferred_element_type=jnp.float32)
                upd = pv if upd is None else upd + pv
            if mode == "diag":
                acc_ref[p] = upd
            else:
                acc_ref[p] += upd

    def still_visible():
        return jnp.min(carry_ref[...]) < -EXP_ZERO_BELOW

    first_full = LEAD // TILE
    all_pairs(qt, "diag", fill=True)
    all_pairs(qt - 1, "prev", fill=True)
    while pending:
        pending.pop(0)()

    def cond(state):
        j, go = state
        return (j >= first_full) & go

    def body(state):
        j, _ = state
        all_pairs(j, "full")
        return j - 1, still_visible()

    j_end, go = lax.while_loop(cond, body, (qt - 2, still_visible()))

    @pl.when(go & (qt > first_full))
    def _():
        all_pairs(first_full - 1, "lead")


_SB_SCRATCH = [pltpu.VMEM((N_PAIRS, TILE, LANES), jnp.float32),
               pltpu.VMEM((N_PAIRS, 2, TILE, LANES), jnp.float32),
               pltpu.VMEM((N_PAIRS, 2, TILE, TILE), jnp.bfloat16),
               pltpu.VMEM((N_PAIRS, 2, TILE, TILE), jnp.float32),
               pltpu.VMEM((N_PAIRS, 2, TILE, LANES), jnp.float32),
               pltpu.VMEM((N_PAIRS, 2, TILE, TILE), jnp.bfloat16)]


def _t5_bucket_table():
    r = np.arange(SW_BLOCK)[:, None]
    l = np.arange(SW_BLOCK)[None, :]
    n = np.where(l > r, SW_BLOCK + r - l, r - l)
    assert WINDOW == SW_BLOCK and n.min() >= 0 and n.max() < WINDOW
    max_exact = N_BUCKETS // 2
    large = max_exact + (np.log(np.maximum(n, 1) / max_exact)
                         / math.log(MAX_DISTANCE / max_exact)
                         * (N_BUCKETS - max_exact)).astype(np.int32)
    large = np.minimum(large, N_BUCKETS - 1)
    return np.where(n < max_exact, n, large).astype(np.int32)


_SW_GROUP = N_HEADS // (SW_KV_WIDTH // HEAD_DIM)
_SW_STACKS = tuple(
    tuple(h for par in range(2) for h in range(N_HEADS)
          if h % 2 == par and (0 if (h // _SW_GROUP) == par else 1) == variant)
    for variant in range(2))
_SW_STACK_ROWS = len(_SW_STACKS[0]) * SW_BLOCK


def _swa_bias_init(relb_ref, bucket_ref, bias_ref):
    @pl.when((pl.program_id(0) == 0) & (pl.program_id(1) == 0))
    def _():
        bucket = bucket_ref[...]
        for h in range(N_HEADS):
            t = jnp.zeros(bucket.shape, jnp.float32)
            for bkt in range(N_BUCKETS):
                t = jnp.where(bucket == bkt, relb_ref[bkt, h], t)
            bias_ref[h] = t


def _swa_block(sb, sink_ref, q_ref, kvp_ref, kvc_ref, bias_ref, ob_ref):
    i = pl.program_id(1)
    head_lanes = _head_lanes()
    lane = lax.broadcasted_iota(jnp.int32, (SW_BLOCK, SW_BLOCK), 1)
    from_prev = lane > lax.broadcasted_iota(jnp.int32, (SW_BLOCK, SW_BLOCK), 0)
    lane1 = lax.broadcasted_iota(jnp.int32, (1, SW_BLOCK), 1)
    lead_pen = jnp.where((i == 0) & (lane1 < FIRST_VALID - (LEAD - SW_BLOCK)), -jnp.inf, 0.0)
    ones_cols = [jnp.broadcast_to(jnp.where(m, 1.0, 0.0).astype(jnp.bfloat16), (2 * SW_BLOCK, LANES))
                 for m in head_lanes]
    n_half = len(_SW_STACKS[0]) // 2
    rows = slice(sb * SW_BLOCK, (sb + 1) * SW_BLOCK)

    def kv_rows(blk):
        cur = kvc_ref[0, blk, :(sb + 1) * SW_BLOCK, :]
        if sb == 0:
            return jnp.concatenate([kvp_ref[0, blk], cur], axis=0)
        return cur[(sb - 1) * SW_BLOCK:]

    num = [None] * N_PAIRS
    den = [None] * N_PAIRS
    sink_term = [None] * N_HEADS
    for variant, heads in enumerate(_SW_STACKS):
        k2 = kv_rows(variant)
        v2 = kv_rows(2 + variant)
        qs = []
        for h in heads:
            q2 = q_ref[0, h // 2, rows, :]
            qs.append(jnp.where(head_lanes[h % 2], q2, jnp.zeros_like(q2)))
        logits = lax.dot_general(jnp.concatenate(qs, axis=0), k2, _NT,
                                 preferred_element_type=jnp.float32)
        weights = []
        for slot, h in enumerate(heads):
            lg = logits[slot * SW_BLOCK:(slot + 1) * SW_BLOCK]
            prev = lg[:, :SW_BLOCK]
            if sb == 0:
                prev = prev + lead_pen
            folded = jnp.where(from_prev, prev, lg[:, SW_BLOCK:]) + bias_ref[h]
            m = jnp.maximum(jnp.max(folded, axis=-1, keepdims=True), sink_ref[h])
            m = jnp.broadcast_to(m, folded.shape)
            e = jnp.exp(folded - m)
            sink_term[h] = jnp.exp(sink_ref[h] - m)
            zero = jnp.zeros_like(e)
            weights.append(jnp.concatenate(
                [jnp.where(from_prev, e, zero), jnp.where(from_prev, zero, e)],
                axis=1).astype(jnp.bfloat16))
        for par in range(2):
            vm = jnp.where(head_lanes[par], v2, jnp.zeros_like(v2))
            res = jnp.dot(jnp.concatenate(weights[par * n_half:(par + 1) * n_half], axis=0),
                          jnp.concatenate([vm, ones_cols[par]], axis=1),
                          preferred_element_type=jnp.float32)
            for s in range(n_half):
                p = heads[par * n_half + s] // 2
                part = res[s * SW_BLOCK:(s + 1) * SW_BLOCK]
                num[p] = part[:, :LANES] if num[p] is None else num[p] + part[:, :LANES]
                den[p] = part[:, LANES:] if den[p] is None else den[p] + part[:, LANES:]
    for p in range(N_PAIRS):
        denom = den[p] + jnp.where(head_lanes[0], sink_term[2 * p], sink_term[2 * p + 1])
        ob_ref[rows, p * LANES:(p + 1) * LANES] = num[p] / denom


def _mix_kernel(relb_ref, sink_ref, bucket_ref, tile_ref, ka_lead_ref, va_lead_ref, kvp_ref,
                x_ref, nw_ref, wg_ref, wb_ref, wo_ref, o_ref,
                bias_ref, xn_ref, act_ref, ob_ref, ka_ref, va_ref, oa_ref, *sb_scratch):
    i = pl.program_id(1)
    lead_tile = LEAD // TILE - 1
    qa_ref, ka_tile_ref, va_tile_ref, qb_ref = (
        tile_ref.at[:, blk:blk + N_PAIRS] for blk in (BLK_QA, BLK_KA, BLK_VA, BLK_QB))
    kvc_ref = tile_ref.at[:, BLK_KVB:BLK_KVB + 4]

    @pl.when(i == 0)
    def _():
        ka_ref[0, :, lead_tile * TILE:LEAD, :] = ka_lead_ref[0]
        va_ref[0, :, lead_tile * TILE:LEAD, :] = va_lead_ref[0]

    own_rows = pl.ds(pl.multiple_of((i + LEAD // TILE) * TILE, TILE), TILE)
    ka_ref[0, :, own_rows, :] = ka_tile_ref[0]
    va_ref[0, :, own_rows, :] = va_tile_ref[0]
    _swa_bias_init(relb_ref, bucket_ref, bias_ref)
    xn_ref[...] = _rms_rows(x_ref[0], nw_ref[...]).astype(jnp.bfloat16)
    chunk = W_CHUNK
    bw = BRANCH_WIDTH

    def gate_chunk(c):
        def run():
            cols = slice(c * chunk, (c + 1) * chunk)
            a = jnp.dot(xn_ref[...], wg_ref[:, cols], preferred_element_type=jnp.float32)
            s = jax.nn.sigmoid(a)
            act_ref[:, cols] = a * s if c * chunk < 2 * bw else s
        return run

    _sb_tile(qa_ref, ka_ref, va_ref, oa_ref, *sb_scratch,
             fillers=[gate_chunk(c) for c in range(GATE_WIDTH // chunk)])
    for sb in range(TILE // SW_BLOCK):
        _swa_block(sb, sink_ref, qb_ref, kvp_ref, kvc_ref, bias_ref, ob_ref)
    oa = jnp.concatenate([oa_ref[p] for p in range(N_PAIRS)], axis=1)
    merged = None
    for g, o_branch in enumerate((oa, ob_ref[...])):
        u = (o_branch * act_ref[:, g * bw:(g + 1) * bw]).astype(jnp.bfloat16)
        y = jnp.dot(u, wb_ref[g], preferred_element_type=jnp.float32)
        term = act_ref[:, 2 * bw + g * D_MODEL:2 * bw + (g + 1) * D_MODEL] * y
        merged = term if merged is None else merged + term
    o_ref[0] = x_ref[0] + jnp.dot(merged.astype(jnp.bfloat16), wo_ref[...],
                                  preferred_element_type=jnp.float32)


def _mix(qkv, x, rel_bias, sinks, norm_w, w_gate, w_branch, w_out):
    b, seq, d = x.shape
    assert all(len(s) == N_HEADS // 2 for s in _SW_STACKS)
    bucket = jnp.asarray(_t5_bucket_table())
    smem = pl.BlockSpec(memory_space=pltpu.SMEM)
    sub = TILE // SW_BLOCK
    lead_tiles = LEAD // TILE
    lp = seq + LEAD
    once = dict(pipeline_mode=pl.Buffered(1))
    const2 = lambda bi, i: (0, 0)
    lead_blocks = lambda blk: pl.BlockSpec((1, N_PAIRS, TILE, LANES),
                                           lambda bi, i: (bi, blk // N_PAIRS, lead_tiles - 1, 0))
    return pl.pallas_call(
        _mix_kernel,
        grid=(b, seq // TILE),
        in_specs=[
            smem, smem,
            pl.BlockSpec((SW_BLOCK, SW_BLOCK), const2, **once),
            pl.BlockSpec((1, QKV_BLOCKS, TILE, LANES), lambda bi, i: (bi, 0, i + lead_tiles, 0)),
            lead_blocks(BLK_KA), lead_blocks(BLK_VA),
            pl.BlockSpec((1, 4, SW_BLOCK, LANES),
                         lambda bi, i: (bi, BLK_KVB // 4, (i + lead_tiles) * sub - 1, 0)),
            pl.BlockSpec((1, TILE, d), lambda bi, i: (bi, i, 0)),
            pl.BlockSpec((1, d), const2, **once),
            pl.BlockSpec((d, GATE_WIDTH), const2, **once),
            pl.BlockSpec((2, BRANCH_WIDTH, d), lambda bi, i: (0, 0, 0), **once),
            pl.BlockSpec((d, d), const2, **once),
        ],
        out_specs=pl.BlockSpec((1, TILE, d), lambda bi, i: (bi, i, 0)),
        out_shape=jax.ShapeDtypeStruct((b, seq, d), jnp.float32),
        scratch_shapes=[pltpu.VMEM((N_HEADS, SW_BLOCK, SW_BLOCK), jnp.float32),
                        pltpu.VMEM((TILE, d), jnp.bfloat16),
                        pltpu.VMEM((TILE, GATE_WIDTH), jnp.float32),
                        pltpu.VMEM((TILE, BRANCH_WIDTH), jnp.float32),
                        pltpu.VMEM((1, N_PAIRS, lp, LANES), jnp.bfloat16),
                        pltpu.VMEM((1, N_PAIRS, lp, LANES), jnp.bfloat16),
                        ] + _SB_SCRATCH,
        compiler_params=pltpu.CompilerParams(
            dimension_semantics=("arbitrary", "arbitrary"), vmem_limit_bytes=VMEM_LIMIT),
        name="attend_mix",
    )(rel_bias, sinks, bucket, *([qkv] * 4), x, norm_w,
      w_gate, w_branch, w_out)


def kernel(x, meta, rel_bias, norm_w, w_in, q_gain, k_gain, sinks, w_branch, w_out):
    b, seq, d = x.shape
    assert d == D_MODEL and norm_w.shape[0] == 1, "single-layer block of width 1024 only"
    assert meta.shape == (N_META, d) and seq % ROW_TILE == 0
    assert w_in.shape == (1, d, 4 * BRANCH_WIDTH + 2 * SW_KV_WIDTH + GATE_WIDTH)
    gq = jnp.tile(q_gain[0], 4)[None, :] * (HEAD_DIM ** -0.5)
    gk = jnp.tile(k_gain[0], 4)[None, :]
    nw = norm_w[0][None, :]

    assert w_branch.shape == (1, 2, BRANCH_WIDTH, d) and w_out.shape == (1, d, d)
    qkv, w_gate_bf, w_branch_bf, w_out_bf = _qkv_project(
        x, meta.astype(x.dtype), nw, w_in, gq, gk, w_branch[0].reshape(d, d), w_out[0])
    return _mix(qkv, x, rel_bias, sinks[0], nw, w_gate_bf,
                w_branch_bf.reshape(2, BRANCH_WIDTH, d), w_out_bf)
```

```python
import math

import jax
import jax.numpy as jnp
import numpy as np
from jax import lax
from jax.experimental import pallas as pl
from jax.experimental.pallas import tpu as pltpu

D_MODEL = 1024
HEAD_DIM = 64
BRANCH_WIDTH = 512
SW_KV_WIDTH = 128
N_HEADS = BRANCH_WIDTH // HEAD_DIM
N_META = 16
WINDOW = 128
SW_BLOCK = 128
N_BUCKETS = 32
MAX_DISTANCE = 128
RMS_EPS = 1e-6

LANES = 128
TILE = 256
LEAD = 2 * TILE
FIRST_VALID = LEAD - N_META
ROW_TILE = 512
N_PAIRS = BRANCH_WIDTH // LANES
QKV_PROJ_WIDTH = 4 * BRANCH_WIDTH + 2 * SW_KV_WIDTH
QKV_BLOCKS = 4 * N_PAIRS + 4
BLK_QA, BLK_KA, BLK_VA, BLK_QB, BLK_KVB = 0, N_PAIRS, 2 * N_PAIRS, 3 * N_PAIRS, 4 * N_PAIRS
GATE_WIDTH = 2 * BRANCH_WIDTH + 2 * D_MODEL
W_CHUNK = 2 * LANES
_QKV_SRC_CHUNKS = (0, 1, 2, 3, 4, 5, 8, 9, 10)
_QKV_PIECES = ((0, 6), (8, 4))
_GATE_COLS = ((6 * W_CHUNK, 8 * W_CHUNK), (11 * W_CHUNK, 21 * W_CHUNK))
EXP_ZERO_BELOW = -105.0
VMEM_LIMIT = 56 * 1024 * 1024

_NT = (((1,), (1,)), ((), ()))


def _rms_rows(x, w):
    ms = jnp.mean(x * x, axis=-1, keepdims=True)
    return x * lax.rsqrt(ms + RMS_EPS) * w


def _exp_neg_abs(x):
    return jnp.exp2(jnp.abs(x) * (-math.log2(math.e)))


def _head_lanes():
    lane = lax.broadcasted_iota(jnp.int32, (1, LANES), 1)
    return lane < HEAD_DIM, lane >= HEAD_DIM


def _w_chunk(pieces, c):
    for ref, first, count in pieces:
        if first <= c < first + count:
            off = (c - first) * W_CHUNK
            return ref[0, :, off:off + W_CHUNK].astype(jnp.bfloat16)
    raise ValueError(c)


def _w_piece_spec(d, first, count):
    assert first % count == 0
    return pl.BlockSpec((1, d, count * W_CHUNK), lambda bi, i: (0, 0, first // count),
                        pipeline_mode=pl.Buffered(1))


def _qkv_kernel(x_ref, meta_ref, nw_ref, wa_ref, wb_ref, gq_ref, gk_ref, win_ref, wbr_ref, wo_ref,
                o_ref, wg_bf_ref, wbr_bf_ref, wo_bf_ref):
    i = pl.program_id(1)
    chunk = W_CHUNK
    def cast_weight_slabs():
        wg_bf_ref[...] = jnp.concatenate([win_ref[0, :, lo:hi] for lo, hi in _GATE_COLS],
                                         axis=1).astype(wg_bf_ref.dtype)
        wbr_bf_ref[...] = wbr_ref[...].astype(wbr_bf_ref.dtype)
        wo_bf_ref[...] = wo_ref[...].astype(wo_bf_ref.dtype)

    pieces =[(wa_ref,) + _QKV_PIECES[0], (wb_ref,) + _QKV_PIECES[1]]
    r = lax.broadcasted_iota(jnp.int32, (chunk, chunk), 0) // HEAD_DIM
    c = lax.broadcasted_iota(jnp.int32, (chunk, chunk), 1) // HEAD_DIM
    head_ones = jnp.where(r == c, 1.0, 0.0).astype(jnp.bfloat16)

    def head_norm(a, gain):
        ssq = jnp.dot((a * a).astype(jnp.bfloat16), head_ones, preferred_element_type=jnp.float32)
        return a * lax.rsqrt(ssq * (1.0 / HEAD_DIM) + RMS_EPS) * gain

    def project(rows, r0, nrows):
        def emit(blk, a):
            o_ref[0, blk, r0:r0 + nrows, :] = a.astype(o_ref.dtype)

        xn = _rms_rows(rows, nw_ref[...]).astype(jnp.bfloat16)
        n_chunks = QKV_PROJ_WIDTH // chunk
        first_norm = BLK_QB // 2
        proj = lambda j: jnp.dot(xn, _w_chunk(pieces, _QKV_SRC_CHUNKS[j]),
                                 preferred_element_type=jnp.float32)
        to_norm = {j: proj(j) for j in range(first_norm, n_chunks)}
        for j in range(first_norm):
            a = proj(j)
            if 2 * j < BLK_KA:
                a = a * (HEAD_DIM ** -0.5)
            emit(2 * j, a[:, :LANES])
            emit(2 * j + 1, a[:, LANES:])
        for j, a in to_norm.items():
            if 2 * j < BLK_KVB:
                a = head_norm(a, gq_ref[...])
                emit(2 * j, a[:, :LANES])
                emit(2 * j + 1, a[:, LANES:])
            else:
                kb = head_norm(a, gk_ref[...])[:, :LANES]
                vb = a[:, LANES:]
                emit(BLK_KVB, kb)
                emit(BLK_KVB + 1, pltpu.roll(kb, HEAD_DIM, 1))
                emit(BLK_KVB + 2, vb)
                emit(BLK_KVB + 3, pltpu.roll(vb, HEAD_DIM, 1))

    @pl.when(i == 0)
    def _():
        o_ref[0, :, :FIRST_VALID, :] = jnp.zeros((QKV_BLOCKS, FIRST_VALID, LANES), o_ref.dtype)
        project(meta_ref[...], FIRST_VALID, N_META)
        cast_weight_slabs()

    @pl.when(i > 0)
    def _():
        project(x_ref[0], 0, ROW_TILE)
        cast_weight_slabs()


def _qkv_project(x, meta, norm_w, w_in, gq, gk, w_branch, w_out):
    b, seq, d = x.shape
    assert LEAD == ROW_TILE and seq % ROW_TILE == 0
    n_tiles = (seq + LEAD) // ROW_TILE
    const = lambda bi, i: (0, 0)
    slab = d // (n_tiles - 1)
    assert slab * (n_tiles - 1) == d and w_branch.shape == (d, d) and w_out.shape == (d, d)
    slab_of = lambda bi, i: jnp.where(bi == 0, jnp.maximum(i - 1, 0), n_tiles - 2)
    slab_spec = pl.BlockSpec((slab, d), lambda bi, i: (slab_of(bi, i), 0))
    gate_slab_spec = pl.BlockSpec((slab, GATE_WIDTH), lambda bi, i: (slab_of(bi, i), 0))
    w_in_slab_spec = pl.BlockSpec((1, slab, w_in.shape[2]), lambda bi, i: (0, slab_of(bi, i), 0))
    return pl.pallas_call(
        _qkv_kernel,
        grid=(b, n_tiles),
        in_specs=[
            pl.BlockSpec((1, ROW_TILE, d), lambda bi, i: (bi, jnp.maximum(i - 1, 0), 0)),
            pl.BlockSpec((N_META, d), const),
            pl.BlockSpec((1, d), const),
            _w_piece_spec(d, *_QKV_PIECES[0]),
            _w_piece_spec(d, *_QKV_PIECES[1]),
            pl.BlockSpec((1, 2 * LANES), const),
            pl.BlockSpec((1, 2 * LANES), const),
            w_in_slab_spec, slab_spec, slab_spec,
        ],
        out_specs=[pl.BlockSpec((1, QKV_BLOCKS, ROW_TILE, LANES), lambda bi, i: (bi, 0, i, 0)),
                   gate_slab_spec, slab_spec, slab_spec],
        out_shape=[jax.ShapeDtypeStruct((b, QKV_BLOCKS, seq + LEAD, LANES), jnp.bfloat16),
                   jax.ShapeDtypeStruct((d, GATE_WIDTH), jnp.bfloat16),
                   jax.ShapeDtypeStruct((d, d), jnp.bfloat16),
                   jax.ShapeDtypeStruct((d, d), jnp.bfloat16)],
        compiler_params=pltpu.CompilerParams(
            dimension_semantics=("arbitrary", "arbitrary"), vmem_limit_bytes=VMEM_LIMIT),
        name="qkv_project",
    )(x, meta, norm_w, w_in, w_in, gq, gk, w_in, w_branch, w_out)


def _sb_tile(q_ref, k_ref, v_ref, acc_ref, carry_ref, negl_ref, logsig_ref, first_ref, w_ref,
             fillers=()):
    pending = list(fillers)
    qt = pl.program_id(1) + LEAD // TILE
    head_lanes = _head_lanes()
    row = lax.broadcasted_iota(jnp.int32, (TILE, TILE), 0)
    col = lax.broadcasted_iota(jnp.int32, (TILE, TILE), 1)
    later_sum = jnp.where(row > col, 1.0, 0.0).astype(jnp.bfloat16)
    masks = {"diag": row > col, "lead": col >= FIRST_VALID - TILE, "full": None,
             "prev": col >= FIRST_VALID - (qt - 1) * TILE}

    def all_pairs(j, mode, fill=False):
        mask = masks[mode]
        start = pl.multiple_of(j * TILE, TILE)
        for p in range(N_PAIRS):
            q2 = q_ref[0, p]
            k2 = k_ref[0, p, pl.ds(start, TILE), :]
            for h in range(2):
                qm = jnp.where(head_lanes[h], q2, jnp.zeros_like(q2))
                z = lax.dot_general(qm, k2, _NT, preferred_element_type=jnp.float32)
                neg_l = jnp.maximum(z, 0.0) + jnp.log(1.0 + _exp_neg_abs(z))
                if mask is not None:
                    neg_l = jnp.where(mask, neg_l, 0.0)
                negl_ref[p, h] = neg_l.astype(jnp.bfloat16)
                logsig_ref[p, h] = z - neg_l
                first_ref[p, h] = neg_l[:, :LANES]
                if fill and pending:
                    pending.pop(0)()
        for p in range(N_PAIRS):
            for h in range(2):
                later = jnp.dot(negl_ref[p, h], later_sum, preferred_element_type=jnp.float32)
                total = jnp.broadcast_to(later[:, :1] + first_ref[p, h][:, :1], (TILE, LANES))
                logw = logsig_ref[p, h] - later
                if mode != "diag":
                    carry = carry_ref[p, h]
                    logw = logw - jnp.concatenate([carry] * (TILE // LANES), axis=1)
                    carry_ref[p, h] = carry + total
                else:
                    carry_ref[p, h] = total
                w = jnp.exp(logw)
                if mask is not None:
                    w = jnp.where(mask, w, 0.0)
                w_ref[p, h] = w.astype(jnp.bfloat16)
        for p in range(N_PAIRS):
            v2 = v_ref[0, p, pl.ds(start, TILE), :]
            upd = None
            for h in range(2):
                vm = jnp.where(head_lanes[h], v2, jnp.zeros_like(v2))
                pv = jnp.dot(w_ref[p, h], vm, preferred_element_type=jnp.float32)
                upd = pv if upd is None else upd + pv
            if mode == "diag":
                acc_ref[p] = upd
            else:
                acc_ref[p] += upd

    def still_visible():
        return jnp.min(carry_ref[...]) < -EXP_ZERO_BELOW

    first_full = LEAD // TILE
    all_pairs(qt, "diag", fill=True)
    all_pairs(qt - 1, "prev", fill=True)
    while pending:
        pending.pop(0)()

    def cond(state):
        j, go = state
        return (j >= first_full) & go

    def body(state):
        j, _ = state
        all_pairs(j, "full")
        return j - 1, still_visible()

    j_end, go = lax.while_loop(cond, body, (qt - 2, still_visible()))

    @pl.when(go & (qt > first_full))
    def _():
        all_pairs(first_full - 1, "lead")


_SB_SCRATCH = [pltpu.VMEM((N_PAIRS, TILE, LANES), jnp.float32),
               pltpu.VMEM((N_PAIRS, 2, TILE, LANES), jnp.float32),
               pltpu.VMEM((N_PAIRS, 2, TILE, TILE), jnp.bfloat16),
               pltpu.VMEM((N_PAIRS, 2, TILE, TILE), jnp.float32),
               pltpu.VMEM((N_PAIRS, 2, TILE, LANES), jnp.float32),
               pltpu.VMEM((N_PAIRS, 2, TILE, TILE), jnp.bfloat16)]


def _t5_bucket_table():
    r = np.arange(SW_BLOCK)[:, None]
    l = np.arange(SW_BLOCK)[None, :]
    n = np.where(l > r, SW_BLOCK + r - l, r - l)
    assert WINDOW == SW_BLOCK and n.min() >= 0 and n.max() < WINDOW
    max_exact = N_BUCKETS // 2
    large = max_exact + (np.log(np.maximum(n, 1) / max_exact)
                         / math.log(MAX_DISTANCE / max_exact)
                         * (N_BUCKETS - max_exact)).astype(np.int32)
    large = np.minimum(large, N_BUCKETS - 1)
    return np.where(n < max_exact, n, large).astype(np.int32)


_SW_GROUP = N_HEADS // (SW_KV_WIDTH // HEAD_DIM)
_SW_STACKS = tuple(
    tuple(h for par in range(2) for h in range(N_HEADS)
          if h % 2 == par and (0 if (h // _SW_GROUP) == par else 1) == variant)
    for variant in range(2))
_SW_STACK_ROWS = len(_SW_STACKS[0]) * SW_BLOCK


def _swa_bias_init(relb_ref, bucket_ref, bias_ref):
    @pl.when((pl.program_id(0) == 0) & (pl.program_id(1) == 0))
    def _():
        bucket = bucket_ref[...]
        for h in range(N_HEADS):
            t = jnp.zeros(bucket.shape, jnp.float32)
            for bkt in range(N_BUCKETS):
                t = jnp.where(bucket == bkt, relb_ref[bkt, h], t)
            bias_ref[h] = t


def _swa_block(sb, sink_ref, q_ref, kvp_ref, kvc_ref, bias_ref, ob_ref):
    i = pl.program_id(1)
    head_lanes = _head_lanes()
    lane = lax.broadcasted_iota(jnp.int32, (SW_BLOCK, SW_BLOCK), 1)
    from_prev = lane > lax.broadcasted_iota(jnp.int32, (SW_BLOCK, SW_BLOCK), 0)
    lane1 = lax.broadcasted_iota(jnp.int32, (1, SW_BLOCK), 1)
    lead_pen = jnp.where((i == 0) & (lane1 < FIRST_VALID - (LEAD - SW_BLOCK)), -jnp.inf, 0.0)
    ones_cols = [jnp.broadcast_to(jnp.where(m, 1.0, 0.0).astype(jnp.bfloat16), (2 * SW_BLOCK, LANES))
                 for m in head_lanes]
    n_half = len(_SW_STACKS[0]) // 2
    rows = slice(sb * SW_BLOCK, (sb + 1) * SW_BLOCK)

    def kv_rows(blk):
        cur = kvc_ref[0, blk, :(sb + 1) * SW_BLOCK, :]
        if sb == 0:
            return jnp.concatenate([kvp_ref[0, blk], cur], axis=0)
        return cur[(sb - 1) * SW_BLOCK:]

    num = [None] * N_PAIRS
    den = [None] * N_PAIRS
    sink_term = [None] * N_HEADS
    for variant, heads in enumerate(_SW_STACKS):
        k2 = kv_rows(variant)
        v2 = kv_rows(2 + variant)
        qs = []
        for h in heads:
            q2 = q_ref[0, h // 2, rows, :]
            qs.append(jnp.where(head_lanes[h % 2], q2, jnp.zeros_like(q2)))
        logits = lax.dot_general(jnp.concatenate(qs, axis=0), k2, _NT,
                                 preferred_element_type=jnp.float32)
        weights = []
        for slot, h in enumerate(heads):
            lg = logits[slot * SW_BLOCK:(slot + 1) * SW_BLOCK]
            prev = lg[:, :SW_BLOCK]
            if sb == 0:
                prev = prev + lead_pen
            folded = jnp.where(from_prev, prev, lg[:, SW_BLOCK:]) + bias_ref[h]
            m = jnp.maximum(jnp.max(folded, axis=-1, keepdims=True), sink_ref[h])
            m = jnp.broadcast_to(m, folded.shape)
            e = jnp.exp(folded - m)
            sink_term[h] = jnp.exp(sink_ref[h] - m)
            zero = jnp.zeros_like(e)
            weights.append(jnp.concatenate(
                [jnp.where(from_prev, e, zero), jnp.where(from_prev, zero, e)],
                axis=1).astype(jnp.bfloat16))
        for par in range(2):
            vm = jnp.where(head_lanes[par], v2, jnp.zeros_like(v2))
            res = jnp.dot(jnp.concatenate(weights[par * n_half:(par + 1) * n_half], axis=0),
                          jnp.concatenate([vm, ones_cols[par]], axis=1),
                          preferred_element_type=jnp.float32)
            for s in range(n_half):
                p = heads[par * n_half + s] // 2
                part = res[s * SW_BLOCK:(s + 1) * SW_BLOCK]
                num[p] = part[:, :LANES] if num[p] is None else num[p] + part[:, :LANES]
                den[p] = part[:, LANES:] if den[p] is None else den[p] + part[:, LANES:]
    for p in range(N_PAIRS):
        denom = den[p] + jnp.where(head_lanes[0], sink_term[2 * p], sink_term[2 * p + 1])
        ob_ref[rows, p * LANES:(p + 1) * LANES] = num[p] / denom


def _mix_kernel(relb_ref, sink_ref, bucket_ref, tile_ref, ka_lead_ref, va_lead_ref, kvp_ref,
                x_ref, nw_ref, wg_ref, wb_ref, wo_ref, o_ref,
                bias_ref, xn_ref, act_ref, ob_ref, ka_ref, va_ref, oa_ref, *sb_scratch):
    i = pl.program_id(1)
    lead_tile = LEAD // TILE - 1
    qa_ref, ka_tile_ref, va_tile_ref, qb_ref = (
        tile_ref.at[:, blk:blk + N_PAIRS] for blk in (BLK_QA, BLK_KA, BLK_VA, BLK_QB))
    kvc_ref = tile_ref.at[:, BLK_KVB:BLK_KVB + 4]

    @pl.when(i == 0)
    def _():
        ka_ref[0, :, lead_tile * TILE:LEAD, :] = ka_lead_ref[0]
        va_ref[0, :, lead_tile * TILE:LEAD, :] = va_lead_ref[0]

    _swa_bias_init(relb_ref, bucket_ref, bias_ref)
    own_rows = pl.ds(pl.multiple_of((i + LEAD // TILE) * TILE, TILE), TILE)
    ka_ref[0, :, own_rows, :] = ka_tile_ref[0]
    va_ref[0, :, own_rows, :] = va_tile_ref[0]
    xn_ref[...] = _rms_rows(x_ref[0], nw_ref[...]).astype(jnp.bfloat16)
    chunk = W_CHUNK
    bw = BRANCH_WIDTH

    def gate_chunk(c):
        def run():
            cols = slice(c * chunk, (c + 1) * chunk)
            a = jnp.dot(xn_ref[...], wg_ref[:, cols], preferred_element_type=jnp.float32)
            s = jax.nn.sigmoid(a)
            act_ref[:, cols] = a * s if c * chunk < 2 * bw else s
        return run

    _sb_tile(qa_ref, ka_ref, va_ref, oa_ref, *sb_scratch,
             fillers=[gate_chunk(c) for c in range(GATE_WIDTH // chunk)])
    for sb in range(TILE // SW_BLOCK):
        _swa_block(sb, sink_ref, qb_ref, kvp_ref, kvc_ref, bias_ref, ob_ref)
    oa = jnp.concatenate([oa_ref[p] for p in range(N_PAIRS)], axis=1)
    merged = None
    for g, o_branch in enumerate((oa, ob_ref[...])):
        u = (o_branch * act_ref[:, g * bw:(g + 1) * bw]).astype(jnp.bfloat16)
        y = jnp.dot(u, wb_ref[g], preferred_element_type=jnp.float32)
        term = act_ref[:, 2 * bw + g * D_MODEL:2 * bw + (g + 1) * D_MODEL] * y
        merged = term if merged is None else merged + term
    o_ref[0] = x_ref[0] + jnp.dot(merged.astype(jnp.bfloat16), wo_ref[...],
                                  preferred_element_type=jnp.float32)


def _mix(qkv, x, rel_bias, sinks, norm_w, w_gate, w_branch, w_out):
    b, seq, d = x.shape
    assert all(len(s) == N_HEADS // 2 for s in _SW_STACKS)
    bucket = jnp.asarray(_t5_bucket_table())
    smem = pl.BlockSpec(memory_space=pltpu.SMEM)
    sub = TILE // SW_BLOCK
    lead_tiles = LEAD // TILE
    lp = seq + LEAD
    once = dict(pipeline_mode=pl.Buffered(1))
    const2 = lambda bi, i: (0, 0)
    lead_blocks = lambda blk: pl.BlockSpec((1, N_PAIRS, TILE, LANES),
                                           lambda bi, i: (bi, blk // N_PAIRS, lead_tiles - 1, 0))
    return pl.pallas_call(
        _mix_kernel,
        grid=(b, seq // TILE),
        in_specs=[
            smem, smem,
            pl.BlockSpec((SW_BLOCK, SW_BLOCK), const2, **once),
            pl.BlockSpec((1, QKV_BLOCKS, TILE, LANES), lambda bi, i: (bi, 0, i + lead_tiles, 0)),
            lead_blocks(BLK_KA), lead_blocks(BLK_VA),
            pl.BlockSpec((1, 4, SW_BLOCK, LANES),
                         lambda bi, i: (bi, BLK_KVB // 4, (i + lead_tiles) * sub - 1, 0)),
            pl.BlockSpec((1, TILE, d), lambda bi, i: (bi, i, 0)),
            pl.BlockSpec((1, d), const2, **once),
            pl.BlockSpec((d, GATE_WIDTH), const2, **once),
            pl.BlockSpec((2, BRANCH_WIDTH, d), lambda bi, i: (0, 0, 0), **once),
            pl.BlockSpec((d, d), const2, **once),
        ],
        out_specs=pl.BlockSpec((1, TILE, d), lambda bi, i: (bi, i, 0)),
        out_shape=jax.ShapeDtypeStruct((b, seq, d), jnp.float32),
        scratch_shapes=[pltpu.VMEM((N_HEADS, SW_BLOCK, SW_BLOCK), jnp.float32),
                        pltpu.VMEM((TILE, d), jnp.bfloat16),
                        pltpu.VMEM((TILE, GATE_WIDTH), jnp.float32),
                        pltpu.VMEM((TILE, BRANCH_WIDTH), jnp.float32),
                        pltpu.VMEM((1, N_PAIRS, lp, LANES), jnp.bfloat16),
                        pltpu.VMEM((1, N_PAIRS, lp, LANES), jnp.bfloat16),
                        ] + _SB_SCRATCH,
        compiler_params=pltpu.CompilerParams(
            dimension_semantics=("arbitrary", "arbitrary"), vmem_limit_bytes=VMEM_LIMIT),
        name="attend_mix",
    )(rel_bias, sinks, bucket, *([qkv] * 4), x, norm_w,
      w_gate, w_branch, w_out)


def kernel(x, meta, rel_bias, norm_w, w_in, q_gain, k_gain, sinks, w_branch, w_out):
    b, seq, d = x.shape
    assert d == D_MODEL and norm_w.shape[0] == 1, "single-layer block of width 1024 only"
    assert meta.shape == (N_META, d) and seq % ROW_TILE == 0
    assert w_in.shape == (1, d, 4 * BRANCH_WIDTH + 2 * SW_KV_WIDTH + GATE_WIDTH)
    gq = jnp.tile(q_gain[0], 4)[None, :] * (HEAD_DIM ** -0.5)
    gk = jnp.tile(k_gain[0], 4)[None, :]
    nw = norm_w[0][None, :]

    assert w_branch.shape == (1, 2, BRANCH_WIDTH, d) and w_out.shape == (1, d, d)
    qkv, w_gate_bf, w_branch_bf, w_out_bf = _qkv_project(
        x, meta.astype(x.dtype), nw, w_in, gq, gk, w_branch[0].reshape(d, d), w_out[0])
    return _mix(qkv, x, rel_bias, sinks[0], nw, w_gate_bf,
                w_branch_bf.reshape(2, BRANCH_WIDTH, d), w_out_bf)
```
